```python
import math
import jax, jax.numpy as jnp
from jax import lax
import numpy as np

D_MODEL = 1024
BATCH = 8
SEQ = 2048
DEPTH = 4

N_EVEN = (DEPTH + 1) // 2
N_ODD = DEPTH // 2
MIX_WIDTH = D_MODEL
GLA_HEADS = 4
GLA_WIDTH = MIX_WIDTH // 2
GLA_HEAD_V = GLA_WIDTH // GLA_HEADS
GLA_KEY = GLA_WIDTH // 2
GLA_HEAD_K = GLA_KEY // GLA_HEADS
GATE_RANK = 16
GATE_NORMALIZER = 16.0
GLA_CHUNK = 64
DIFF_HEADS = 4
DIFF_WIDTH = MIX_WIDTH - GLA_WIDTH
DIFF_HEAD_V = DIFF_WIDTH // DIFF_HEADS
DIFF_HEAD_DIM = DIFF_HEAD_V // 2
Q_BLOCK = 128
REL_BUCKETS = 32
REL_MAX_DIST = 128
POOL_WINDOWS = (2, 4, 8, 16)
POOL_GROUPS = 4
POOL_GROUP_W = D_MODEL // POOL_GROUPS
N_EXPERTS = 16
N_EXPERT_GROUPS = 4
EXPERTS_PER_GROUP = N_EXPERTS // N_EXPERT_GROUPS
TOP_K = 2
D_FF = D_MODEL
ALPHA = (2.0 * DEPTH) ** 0.25
BETA = (8.0 * DEPTH) ** -0.25
LN_EPS = 1e-5
RMS_EPS = 1e-6

IN_SIZES = (GLA_KEY, GLA_KEY, GLA_WIDTH, GLA_WIDTH, 2 * GATE_RANK, DIFF_WIDTH, DIFF_WIDTH, DIFF_WIDTH)
IN_COLS = sum(IN_SIZES)
IN_OFFSETS = tuple(sum(IN_SIZES[:i + 1]) for i in range(len(IN_SIZES) - 1))

kernel_name = "hybrid_gla_diffattn_pool_groupedmoe_encoder"


def layer_norm(x, g, b):
    xf = x.astype(jnp.float32)
    mu = jnp.mean(xf, axis=-1, keepdims=True)
    var = jnp.mean(jnp.square(xf - mu), axis=-1, keepdims=True)
    return ((xf - mu) * lax.rsqrt(var + LN_EPS) * g + b).astype(x.dtype)


def rms_norm(x, w):
    xf = x.astype(jnp.float32)
    return xf * lax.rsqrt(jnp.mean(jnp.square(xf), axis=-1, keepdims=True) + RMS_EPS) * w


def t5_bucket(rel):
    nb = REL_BUCKETS // 2
    max_exact = nb // 2
    ret = jnp.where(rel > 0, nb, 0)
    n = jnp.abs(rel)
    nf = jnp.maximum(n, 1).astype(jnp.float32)
    large = max_exact + (jnp.log(nf / max_exact) / math.log(REL_MAX_DIST / max_exact)
                         * (nb - max_exact)).astype(jnp.int32)
    large = jnp.minimum(large, nb - 1)
    return ret + jnp.where(n < max_exact, n, large)


def gla_chunked(q, k, v, log_a, inclusive):
    B, H, S, dk = q.shape
    dv = v.shape[-1]
    L = GLA_CHUNK
    n = S // L
    q = q.reshape(B, H, n, L, dk)
    k = k.reshape(B, H, n, L, dk)
    v = v.reshape(B, H, n, L, dv)
    b = jnp.cumsum(log_a.reshape(B, H, n, L, dk), axis=3)
    b_mid = b[:, :, :, L // 2 - 1:L // 2, :]
    b_last = b[:, :, :, L - 1:L, :]
    scores = jnp.einsum('bhnik,bhnjk->bhnij', q * jnp.exp(b - b_mid), k * jnp.exp(b_mid - b))
    mask = jnp.tril(jnp.ones((L, L), dtype=bool), k=0 if inclusive else -1)
    o_intra = jnp.einsum('bhnij,bhnjv->bhniv', jnp.where(mask, scores, 0.0), v)
    d_state = jnp.einsum('bhnjk,bhnjv->bhnkv', k * jnp.exp(b_last - b), v)
    decay = jnp.exp(b_last[:, :, :, 0, :])

    def step(state, inp):
        dec, ds = inp
        return dec[..., None] * state + ds, state

    init = jnp.zeros((B, H, dk, dv), jnp.float32)
    _, prev_states = lax.scan(step, init, (jnp.moveaxis(decay, 2, 0), jnp.moveaxis(d_state, 2, 0)))
    prev_states = jnp.moveaxis(prev_states, 0, 2)
    o_inter = jnp.einsum('bhnik,bhnkv->bhniv', q * jnp.exp(b), prev_states)
    return (o_intra + o_inter).reshape(B, H, S, dv)


def gla_mixer(q, k, v, g_out, lowrank, w_gate, b_gate, norm_w):
    B, S, _ = q.shape

    def heads(t, d):
        return t.reshape(B, S, GLA_HEADS, d).transpose(0, 2, 1, 3).astype(jnp.float32)

    qh = heads(q, GLA_HEAD_K) * (GLA_HEAD_K ** -0.5)
    kh = heads(k, GLA_HEAD_K)
    vh = heads(v, GLA_HEAD_V)
    lr = lowrank.reshape(B, S, 2, GATE_RANK)
    logit = jnp.einsum('bsdr,drk->bsdk', lr, w_gate) + b_gate
    log_a = jax.nn.log_sigmoid(logit.astype(jnp.float32)) / GATE_NORMALIZER
    la_fwd = heads(log_a[:, :, 0], GLA_HEAD_K)
    la_bwd = heads(log_a[:, :, 1], GLA_HEAD_K)
    o_fwd = gla_chunked(qh, kh, vh, la_fwd, True)
    flip = lambda t: jnp.flip(t, axis=2)
    o_bwd = flip(gla_chunked(flip(qh), flip(kh), flip(vh), flip(la_bwd), False))
    o = rms_norm(o_fwd + o_bwd, norm_w)
    o = o.transpose(0, 2, 1, 3).reshape(B, S, GLA_WIDTH)
    return o * jax.nn.silu(g_out.astype(jnp.float32))


def diff_mixer(q, k, v, lam, norm_w, rel_bias, lam_init):
    B, S, _ = q.shape
    qh = q.reshape(B, S, DIFF_HEADS, 2, DIFF_HEAD_DIM).transpose(3, 0, 2, 1, 4).astype(jnp.float32)
    qh = qh * (DIFF_HEAD_DIM ** -0.5)
    kh = k.reshape(B, S, DIFF_HEADS, 2, DIFF_HEAD_DIM).transpose(3, 0, 2, 1, 4).astype(jnp.float32)
    vh = v.reshape(B, S, DIFF_HEADS, DIFF_HEAD_V).transpose(0, 2, 1, 3).astype(jnp.float32)
    lam = lam.astype(jnp.float32)
    lam_full = jnp.exp(jnp.sum(lam[0] * lam[1])) - jnp.exp(jnp.sum(lam[2] * lam[3])) + lam_init
    n_blocks = S // Q_BLOCK
    q_blocks = qh.reshape(2, B, DIFF_HEADS, n_blocks, Q_BLOCK, DIFF_HEAD_DIM).transpose(3, 0, 1, 2, 4, 5)
    kpos = jnp.arange(S, dtype=jnp.int32)

    def block(args):
        qb, start = args
        qpos = start + jnp.arange(Q_BLOCK, dtype=jnp.int32)
        bucket = t5_bucket(kpos[None, :] - qpos[:, None])
        bias = rel_bias[bucket].transpose(2, 0, 1).astype(jnp.float32)
        s = jnp.einsum('cbhqd,cbhkd->cbhqk', qb, kh) + bias
        p = jax.nn.softmax(s, axis=-1)
        return jnp.einsum('bhqk,bhkv->bhqv', p[0] - lam_full * p[1], vh)

    starts = jnp.arange(n_blocks, dtype=jnp.int32) * Q_BLOCK
    o = lax.map(block, (q_blocks, starts))
    o = o.transpose(1, 2, 0, 3, 4).reshape(B, DIFF_HEADS, S, DIFF_HEAD_V)
    o = rms_norm(o, norm_w) * (1.0 - lam_init)
    return o.transpose(0, 2, 1, 3).reshape(B, S, DIFF_WIDTH)


def pool_mixer(x, w_pool, scale):
    B, S, D = x.shape
    xg = x.astype(jnp.float32).reshape(B, S, POOL_GROUPS, POOL_GROUP_W)
    cs = jnp.pad(jnp.cumsum(xg, axis=1), ((0, 0), (1, 0), (0, 0), (0, 0)))
    half = jnp.array([w // 2 for w in POOL_WINDOWS], dtype=jnp.int32)
    pos = jnp.arange(S, dtype=jnp.int32)[:, None]
    lo = jnp.clip(pos - half[None, :], 0, S)
    hi = jnp.clip(pos + half[None, :], 0, S)
    gi = jnp.arange(POOL_GROUPS, dtype=jnp.int32)[None, :]
    window_sum = cs[:, hi, gi] - cs[:, lo, gi]
    count = (hi - lo).astype(jnp.float32)[None, :, :, None]
    pooled = window_sum / count - xg
    y = jnp.einsum('bsgc,gcd->bsgd', pooled, w_pool).reshape(B, S, D)
    return (y * scale).astype(x.dtype)


def grouped_moe(x, w_router, router_bias, w_gate, w_up, w_down):
    B, S, D = x.shape
    xt = x.reshape(B * S, D)
    aff = jax.nn.sigmoid((xt @ w_router).astype(jnp.float32))
    sel = aff + router_bias
    grouped = sel.reshape(-1, N_EXPERT_GROUPS, EXPERTS_PER_GROUP)
    group_score = jnp.sum(lax.top_k(grouped, TOP_K)[0], axis=-1)
    best_group = jnp.argmax(group_score, axis=-1)
    in_group = (jnp.arange(N_EXPERTS) // EXPERTS_PER_GROUP)[None, :] == best_group[:, None]
    _, idx = lax.top_k(jnp.where(in_group, sel, -jnp.inf), TOP_K)
    w = jnp.take_along_axis(aff, idx, axis=-1)
    w = w / jnp.sum(w, axis=-1, keepdims=True)
    gates = jnp.sum(jax.nn.one_hot(idx, N_EXPERTS, dtype=jnp.float32) * w[..., None], axis=1)
    y = jnp.zeros((B * S, D), jnp.float32)
    for e in range(N_EXPERTS):
        h = jax.nn.silu(xt @ w_gate[e]) * (xt @ w_up[e])
        y = y + gates[:, e:e + 1] * (h @ w_down[e])
    return y.reshape(B, S, D).astype(x.dtype)


def setup_inputs(seed: int = 0) -> dict:
    key = jax.random.key(seed)
    ks = jax.random.split(key, 18)
    nrm = jax.random.normal
    f32 = jnp.float32
    v_scale = jnp.concatenate([
        jnp.full((s,), BETA if name == 'v' else 1.0, f32)
        for s, name in zip(IN_SIZES, ('q', 'k', 'v', 'g', 'lr', 'q', 'k', 'v'))])
    return {
        "x": nrm(ks[0], (BATCH, SEQ, D_MODEL), f32),
        "rel_bias": 0.5 * nrm(ks[1], (REL_BUCKETS, DIFF_HEADS), f32),
        "w_in": nrm(ks[2], (N_EVEN, D_MODEL, IN_COLS), f32) * (D_MODEL ** -0.5) * v_scale,
        "w_gla_gate": nrm(ks[3], (N_EVEN, 2, GATE_RANK, GLA_KEY), f32) * (GATE_RANK ** -0.5),
        "b_gla_gate": 0.1 * nrm(ks[4], (N_EVEN, 2, GLA_KEY), f32),
        "gla_norm": 1.0 + 0.1 * nrm(ks[5], (N_EVEN, GLA_HEAD_V), f32),
        "diff_lambda": 0.1 * nrm(ks[6], (N_EVEN, 4, DIFF_HEAD_DIM), f32),
        "diff_norm": 1.0 + 0.1 * nrm(ks[7], (N_EVEN, DIFF_HEAD_V), f32),
        "w_out_mix": nrm(ks[8], (N_EVEN, MIX_WIDTH, D_MODEL), f32) * (MIX_WIDTH ** -0.5) * BETA,
        "w_pool": nrm(ks[9], (N_ODD, POOL_GROUPS, POOL_GROUP_W, POOL_GROUP_W), f32) * (POOL_GROUP_W ** -0.5) * BETA,
        "pool_scale": 1.0 + 0.1 * nrm(ks[10], (N_ODD, D_MODEL), f32),
        "ln_g": 1.0 + 0.1 * nrm(ks[11], (DEPTH, 2, D_MODEL), f32),
        "ln_b": 0.02 * nrm(ks[12], (DEPTH, 2, D_MODEL), f32),
        "w_router": nrm(ks[13], (D_MODEL, N_EXPERTS), f32) * (D_MODEL ** -0.5),
        "router_bias": 0.01 * nrm(ks[14], (N_EXPERTS,), f32),
        "w_gate": nrm(ks[15], (DEPTH, N_EXPERTS, D_MODEL, D_FF), f32) * (D_MODEL ** -0.5),
        "w_up": nrm(ks[16], (DEPTH, N_EXPERTS, D_MODEL, D_FF), f32) * (D_MODEL ** -0.5) * BETA,
        "w_down": nrm(ks[17], (DEPTH, N_EXPERTS, D_FF, D_MODEL), f32) * (D_FF ** -0.5) * BETA,
    }


def reference(x, rel_bias, w_in, w_gla_gate, b_gla_gate, gla_norm, diff_lambda, diff_norm,
              w_out_mix, w_pool, pool_scale, ln_g, ln_b, w_router, router_bias,
              w_gate, w_up, w_down):
    for layer in range(DEPTH):
        if layer % 2 == 0:
            i = layer // 2
            h = x @ w_in[i]
            gq, gk, gv, gg, glr, dq, dk, dv = jnp.split(h, IN_OFFSETS, axis=-1)
            o_gla = gla_mixer(gq, gk, gv, gg, glr, w_gla_gate[i], b_gla_gate[i], gla_norm[i])
            lam_init = 0.8 - 0.6 * math.exp(-0.3 * layer)
            o_diff = diff_mixer(dq, dk, dv, diff_lambda[i], diff_norm[i], rel_bias, lam_init)
            mix = (jnp.concatenate([o_gla, o_diff], axis=-1).astype(x.dtype) @ w_out_mix[i]).astype(x.dtype)
        else:
            j = layer // 2
            mix = pool_mixer(x, w_pool[j], pool_scale[j])
        x = layer_norm(ALPHA * x + mix, ln_g[layer, 0], ln_b[layer, 0])
        ffn = grouped_moe(x, w_router, router_bias, w_gate[layer], w_up[layer], w_down[layer])
        x = layer_norm(ALPHA * x + ffn, ln_g[layer, 1], ln_b[layer, 1])
    return x
```

```python
import functools
import math

import jax
import jax.numpy as jnp
from jax import lax
from jax.experimental import pallas as pl
from jax.experimental.pallas import tpu as pltpu

F32 = jnp.float32
BF16 = jnp.bfloat16

D_MODEL = 1024
BATCH = 8
SEQ = 2048
DEPTH = 4
TOKENS = BATCH * SEQ

GLA_HEADS = 4
GLA_WIDTH = 512
GLA_HEAD_V = 128
GLA_KEY = 256
GLA_HEAD_K = 64
GATE_RANK = 16
GATE_NORMALIZER = 16.0
GLA_CHUNK = 64
GLA_BLOCK = 256
CHUNKS_PER_BLOCK = GLA_BLOCK // GLA_CHUNK

DIFF_HEADS = 4
DIFF_WIDTH = 512
DIFF_HEAD_V = 128
DIFF_HEAD_DIM = 64
REL_BUCKETS = 32
Q_TILE = 256
KEY_BLOCK = 128
BIAS_SLOTS = 6

POOL_WINDOWS = (2, 4, 8, 16)
POOL_GROUPS = 4
POOL_GROUP_W = 256
POOL_TILE = 128
POOL_WIN = 3 * POOL_TILE

N_EXPERTS = 16
N_EXPERT_GROUPS = 4
EXPERTS_PER_GROUP = 4

ALPHA = (2.0 * DEPTH) ** 0.25
LN_EPS = 1e-5
RMS_EPS = 1e-6

H_COLS = 3072
LR_COLS = 128

V7X_VMEM_LIMIT = 56 * 1024 * 1024


def _params(sem, vmem=None):
    return pltpu.CompilerParams(dimension_semantics=sem, vmem_limit_bytes=vmem)


def _split_bf16(x):
    hi = x.astype(BF16)
    lo = (x - hi.astype(F32)).astype(BF16)
    return hi, lo


def _dot(a, b):
    return jnp.dot(a, b, preferred_element_type=F32)


def _dot_nt(a, b):
    return lax.dot_general(a, b, (((1,), (1,)), ((), ())), preferred_element_type=F32)


def _dot_tn(a, b):
    return lax.dot_general(a, b, (((0,), (0,)), ((), ())), preferred_element_type=F32)


def _layer_norm(z, g, b):
    mu = jnp.mean(z, axis=-1, keepdims=True)
    zc = z - mu
    var = jnp.mean(zc * zc, axis=-1, keepdims=True)
    return zc * lax.rsqrt(var + LN_EPS) * g + b


INPROJ_TM = 512


def _inproj_kernel(x_ref, w_ref, wlr_ref, hb_ref, lr_ref):
    xb = x_ref[...].astype(BF16)
    hb_ref[...] = _dot(xb, w_ref[...]).astype(BF16)
    lr_ref[...] = _dot(xb, wlr_ref[...])


def _inproj(x2, w_main, w_lr):
    return pl.pallas_call(
        _inproj_kernel,
        grid=(TOKENS // INPROJ_TM,),
        in_specs=[
            pl.BlockSpec((INPROJ_TM, D_MODEL), lambda i: (i, 0)),
            pl.BlockSpec((D_MODEL, H_COLS), lambda i: (0, 0)),
            pl.BlockSpec((D_MODEL, LR_COLS), lambda i: (0, 0)),
        ],
        out_specs=[
            pl.BlockSpec((INPROJ_TM, H_COLS), lambda i: (i, 0)),
            pl.BlockSpec((INPROJ_TM, LR_COLS), lambda i: (i, 0)),
        ],
        out_shape=[
            jax.ShapeDtypeStruct((TOKENS, H_COLS), BF16),
            jax.ShapeDtypeStruct((TOKENS, LR_COLS), F32),
        ],
        compiler_params=_params(("parallel",), V7X_VMEM_LIMIT),
        name="inproj",
    )(x2, w_main, w_lr)


def _gla_kernel(q_ref, k_ref, v_ref, g_ref, lr_ref, wg_ref, bg_ref, nw_ref,
                out_ref, oacc_ref, state_ref):
    blk, ch = GLA_BLOCK, GLA_CHUNK
    n_blocks = SEQ // blk
    row = lax.broadcasted_iota(jnp.int32, (blk, blk), 0)
    col = lax.broadcasted_iota(jnp.int32, (blk, blk), 1)
    same_chunk = (row // ch) == (col // ch)
    srow = lax.broadcasted_iota(jnp.int32, (GLA_KEY, GLA_WIDTH), 0) // GLA_HEAD_K
    scol = lax.broadcasted_iota(jnp.int32, (GLA_KEY, GLA_WIDTH), 1) // GLA_HEAD_V
    state_mask = srow == scol
    lane_head = lax.broadcasted_iota(jnp.int32, (blk, 2 * GLA_HEAD_K), 1) // GLA_HEAD_K
    ones_tok = jnp.ones((ch, GLA_HEAD_V), BF16)

    def block_step(r0, direction):
        fwd = direction == 0
        rows = pl.ds(r0, blk)
        lr_hi, lr_lo = _split_bf16(lr_ref[rows, :])
        w_hi, w_lo = _split_bf16(wg_ref[direction])
        logit = (_dot(lr_hi, w_hi) + _dot(lr_lo, w_hi) + _dot(lr_hi, w_lo)
                 + bg_ref[direction:direction + 1, :])
        la = (jnp.minimum(logit, 0.0) - jnp.log(1.0 + jnp.exp(-jnp.abs(logit)))) * (1.0 / GATE_NORMALIZER)
        la_hi, la_lo = _split_bf16(la)
        tri = jnp.where(same_chunk & ((col <= row) if fwd else (col >= row)), 1.0, 0.0).astype(BF16)
        b = _dot(tri, la_hi) + _dot(tri, la_lo)
        b3 = b.reshape(CHUNKS_PER_BLOCK, ch, GLA_KEY)
        mid = ch // 2 - 1 if fwd else ch // 2
        last = ch - 1 if fwd else 0
        b_mid = jnp.broadcast_to(b3[:, mid:mid + 1, :], b3.shape).reshape(blk, GLA_KEY)
        b_last = jnp.broadcast_to(b3[:, last:last + 1, :], b3.shape).reshape(blk, GLA_KEY)
        qf = q_ref[rows, :].astype(F32) * (GLA_HEAD_K ** -0.5)
        kf = k_ref[rows, :].astype(F32)
        vb = v_ref[rows, :]
        qd = (qf * jnp.exp(b - b_mid)).astype(BF16)
        kd = (kf * jnp.exp(b_mid - b)).astype(BF16)
        qe = (qf * jnp.exp(b)).astype(BF16)
        kl = (kf * jnp.exp(b_last - b)).astype(BF16)
        keep = same_chunk & ((col <= row) if fwd else (col > row))
        intra = []
        for h in range(GLA_HEADS):
            pair = slice((h // 2) * 128, (h // 2) * 128 + 128)
            qh = jnp.where(lane_head == (h % 2), qd[:, pair], jnp.zeros_like(qd[:, pair]))
            sc = _dot_nt(qh, kd[:, pair])
            p = jnp.where(keep, sc, 0.0).astype(BF16)
            intra.append(_dot(p, vb[:, h * GLA_HEAD_V:(h + 1) * GLA_HEAD_V]))
        o_intra = jnp.concatenate(intra, axis=1)
        inter = [None] * CHUNKS_PER_BLOCK
        order = range(CHUNKS_PER_BLOCK) if fwd else range(CHUNKS_PER_BLOCK - 1, -1, -1)
        for c in order:
            cs = slice(c * ch, (c + 1) * ch)
            state = state_ref[...]
            inter[c] = _dot(qe[cs, :], state.astype(BF16))
            d_state = _dot_tn(kl[cs, :], vb[cs, :])
            tot = _dot_tn(la_hi[cs, :], ones_tok) + _dot_tn(la_lo[cs, :], ones_tok)
            decay = jnp.exp(tot)
            decay = jnp.concatenate([decay] * GLA_HEADS, axis=1)
            state_ref[...] = state * decay + jnp.where(state_mask, d_state, 0.0)
        return o_intra + jnp.concatenate(inter, axis=0)

    state_ref[...] = jnp.zeros_like(state_ref)

    def fwd_body(i, carry):
        r0 = pl.multiple_of(i * blk, blk)
        oacc_ref[pl.ds(r0, blk), :] = block_step(r0, 0)
        return carry

    lax.fori_loop(0, n_blocks, fwd_body, 0)
    state_ref[...] = jnp.zeros_like(state_ref)

    def bwd_body(i, carry):
        r0 = pl.multiple_of((n_blocks - 1 - i) * blk, blk)
        o = oacc_ref[pl.ds(r0, blk), :] + block_step(r0, 1)
        gate = g_ref[pl.ds(r0, blk), :].astype(F32)
        gate = gate / (1.0 + jnp.exp(-gate))
        outs = []
        for h in range(GLA_HEADS):
            oh = o[:, h * GLA_HEAD_V:(h + 1) * GLA_HEAD_V]
            ms = jnp.mean(oh * oh, axis=-1, keepdims=True)
            outs.append(oh * lax.rsqrt(ms + RMS_EPS) * nw_ref[...])
        out_ref[pl.ds(r0, blk), :] = (jnp.concatenate(outs, axis=1) * gate).astype(BF16)
        return carry

    lax.fori_loop(0, n_blocks, bwd_body, 0)


def _gla(hb, lr, wg_pad, bg, nw):
    return pl.pallas_call(
        _gla_kernel,
        grid=(BATCH,),
        in_specs=[
            pl.BlockSpec((SEQ, GLA_KEY), lambda b: (b, 0)),
            pl.BlockSpec((SEQ, GLA_KEY), lambda b: (b, 1)),
            pl.BlockSpec((SEQ, GLA_WIDTH), lambda b: (b, 1)),
            pl.BlockSpec((SEQ, GLA_WIDTH), lambda b: (b, 2)),
            pl.BlockSpec((SEQ, LR_COLS), lambda b: (b, 0)),
            pl.BlockSpec((2, LR_COLS, GLA_KEY), lambda b: (0, 0, 0)),
            pl.BlockSpec((2, GLA_KEY), lambda b: (0, 0)),
            pl.BlockSpec((1, GLA_HEAD_V), lambda b: (0, 0)),
        ],
        out_specs=pl.BlockSpec((SEQ, GLA_WIDTH), lambda b: (b, 0)),
        out_shape=jax.ShapeDtypeStruct((TOKENS, GLA_WIDTH), BF16),
        scratch_shapes=[
            pltpu.VMEM((SEQ, GLA_WIDTH), F32),
            pltpu.VMEM((GLA_KEY, GLA_WIDTH), F32),
        ],
        compiler_params=_params(("parallel",), V7X_VMEM_LIMIT),
        name="gla",
    )(hb, hb, hb, hb, lr, wg_pad, bg, nw)


_BUCKET_STEPS = (12, 16, 23, 32, 46, 64, 91)


def _bias_kernel(rb_ref, out_ref):
    r = lax.broadcasted_iota(jnp.int32, (Q_TILE, KEY_BLOCK), 0)
    l = lax.broadcasted_iota(jnp.int32, (Q_TILE, KEY_BLOCK), 1)
    for s in range(BIAS_SLOTS):
        if s == 0:
            rel = jnp.full((Q_TILE, KEY_BLOCK), -SEQ, jnp.int32)
        elif s == BIAS_SLOTS - 1:
            rel = jnp.full((Q_TILE, KEY_BLOCK), SEQ, jnp.int32)
        else:
            rel = (s - 2) * KEY_BLOCK + l - r
        n = jnp.abs(rel)
        large = jnp.full_like(n, 8)
        for t in _BUCKET_STEPS:
            large = large + (n >= t).astype(jnp.int32)
        bucket = jnp.where(rel > 0, REL_BUCKETS // 2, 0) + jnp.where(n < 8, n, large)
        for h in range(DIFF_HEADS):
            acc = jnp.zeros((Q_TILE, KEY_BLOCK), F32)
            for bkt in range(REL_BUCKETS):
                acc = jnp.where(bucket == bkt, rb_ref[bkt, h], acc)
            out_ref[h, s] = acc


def _bias_tiles(rel_bias):
    return pl.pallas_call(
        _bias_kernel,
        in_specs=[pl.BlockSpec(memory_space=pltpu.SMEM)],
        out_specs=pl.BlockSpec(memory_space=pltpu.VMEM),
        out_shape=jax.ShapeDtypeStruct((DIFF_HEADS, BIAS_SLOTS, Q_TILE, KEY_BLOCK), F32),
        name="bias_tiles",
    )(rel_bias)


def _diff_kernel(lam_init, q_ref, k_ref, v_ref, bias_ref, lam_ref, nw_ref, out_ref):
    qi = pl.program_id(2)
    lam = lam_ref[...]
    lam_full = (jnp.exp(jnp.sum(lam[0:1] * lam[1:2], axis=-1, keepdims=True))
                - jnp.exp(jnp.sum(lam[2:3] * lam[3:4], axis=-1, keepdims=True)) + lam_init)
    q = q_ref[...]
    k = k_ref[...]
    lane = lax.broadcasted_iota(jnp.int32, q.shape, 1)
    qs = (q.astype(F32) * (DIFF_HEAD_DIM ** -0.5)).astype(BF16)
    zero = jnp.zeros_like(qs)
    blocks_per_q = Q_TILE // KEY_BLOCK
    bias = jnp.concatenate(
        [bias_ref[jnp.clip(kb - blocks_per_q * qi + 2, 0, BIAS_SLOTS - 1)] for kb in range(SEQ // KEY_BLOCK)],
        axis=1)
    probs = []
    for c in range(2):
        qc = jnp.where((lane // DIFF_HEAD_DIM) == c, qs, zero)
        s = _dot_nt(qc, k) + bias
        m = jnp.max(s, axis=-1, keepdims=True)
        e = jnp.exp(s - m)
        probs.append(e * (1.0 / jnp.sum(e, axis=-1, keepdims=True)))
    p = (probs[0] - lam_full * probs[1]).astype(BF16)
    o = _dot(p, v_ref[...])
    ms = jnp.mean(o * o, axis=-1, keepdims=True)
    out_ref[...] = (o * lax.rsqrt(ms + RMS_EPS) * nw_ref[...] * (1.0 - lam_init)).astype(BF16)


def _diff(hb, bias_tiles, lam, nw, lam_init):
    nq = SEQ // Q_TILE
    qcol, kcol, vcol = 1536 // 128, 2048 // 128, 2560 // 128
    return pl.pallas_call(
        functools.partial(_diff_kernel, lam_init),
        grid=(BATCH, DIFF_HEADS, nq),
        in_specs=[
            pl.BlockSpec((Q_TILE, DIFF_HEAD_V), lambda b, h, i: (b * nq + i, qcol + h)),
            pl.BlockSpec((SEQ, DIFF_HEAD_V), lambda b, h, i: (b, kcol + h)),
            pl.BlockSpec((SEQ, DIFF_HEAD_V), lambda b, h, i: (b, vcol + h)),
            pl.BlockSpec((None, BIAS_SLOTS, Q_TILE, KEY_BLOCK), lambda b, h, i: (h, 0, 0, 0)),
            pl.BlockSpec((4, DIFF_HEAD_DIM), lambda b, h, i: (0, 0)),
            pl.BlockSpec((1, DIFF_HEAD_V), lambda b, h, i: (0, 0)),
        ],
        out_specs=pl.BlockSpec((Q_TILE, DIFF_HEAD_V), lambda b, h, i: (b * nq + i, h)),
        out_shape=jax.ShapeDtypeStruct((TOKENS, DIFF_WIDTH), BF16),
        compiler_params=_params(("parallel", "parallel", "parallel"), V7X_VMEM_LIMIT),
        name="diff_attn",
    )(hb, hb, hb, bias_tiles, lam, nw)


OUTPROJ_TM = 512


def _outproj_kernel(og_ref, od_ref, w_ref, x_ref, g_ref, b_ref, out_ref):
    mix = _dot(og_ref[...], w_ref[0:GLA_WIDTH, :]) + _dot(od_ref[...], w_ref[GLA_WIDTH:, :])
    out_ref[...] = _layer_norm(ALPHA * x_ref[...] + mix, g_ref[...], b_ref[...])


def _outproj_ln(o_gla, o_diff, w_out, x2, g, b):
    tm = OUTPROJ_TM
    return pl.pallas_call(
        _outproj_kernel,
        grid=(TOKENS // tm,),
        in_specs=[
            pl.BlockSpec((tm, GLA_WIDTH), lambda i: (i, 0)),
            pl.BlockSpec((tm, DIFF_WIDTH), lambda i: (i, 0)),
            pl.BlockSpec((D_MODEL, D_MODEL), lambda i: (0, 0)),
            pl.BlockSpec((tm, D_MODEL), lambda i: (i, 0)),
            pl.BlockSpec((1, D_MODEL), lambda i: (0, 0)),
            pl.BlockSpec((1, D_MODEL), lambda i: (0, 0)),
        ],
        out_specs=pl.BlockSpec((tm, D_MODEL), lambda i: (i, 0)),
        out_shape=jax.ShapeDtypeStruct((TOKENS, D_MODEL), F32),
        compiler_params=_params(("parallel",), V7X_VMEM_LIMIT),
        name="outproj_ln",
    )(o_gla, o_diff, w_out, x2, g, b)


def _pool_kernel(x_ref, w_ref, sc_ref, g_ref, b_ref, out_ref):
    t = pl.program_id(1)
    r0 = pl.multiple_of(t * POOL_TILE, POOL_TILE)
    w0 = pl.multiple_of(jnp.clip(r0 - POOL_TILE, 0, SEQ - POOL_WIN), POOL_TILE)
    xt = x_ref[pl.ds(r0, POOL_TILE), :]
    xw_hi, xw_lo = _split_bf16(x_ref[pl.ds(w0, POOL_WIN), :])
    pos = r0 + lax.broadcasted_iota(jnp.int32, (POOL_TILE, POOL_WIN), 0)
    key = w0 + lax.broadcasted_iota(jnp.int32, (POOL_TILE, POOL_WIN), 1)
    posc = r0 + lax.broadcasted_iota(jnp.int32, (POOL_TILE, 1), 0)
    ys = []
    for gi, win in enumerate(POOL_WINDOWS):
        half = win // 2
        band = jnp.where((key >= pos - half) & (key < pos + half), 1.0, 0.0).astype(BF16)
        cols = slice(gi * POOL_GROUP_W, (gi + 1) * POOL_GROUP_W)
        wsum = _dot(band, xw_hi[:, cols]) + _dot(band, xw_lo[:, cols])
        count = (jnp.minimum(posc + half, SEQ) - jnp.maximum(posc - half, 0)).astype(F32)
        pooled = wsum / count - xt[:, cols]
        ys.append(_dot(pooled.astype(BF16), w_ref[gi]))
    mix = jnp.concatenate(ys, axis=1) * sc_ref[...]
    out_ref[...] = _layer_norm(ALPHA * xt + mix, g_ref[...], b_ref[...])


def _pool_ln(x3, w_pool, scale, g, b):
    nt = SEQ // POOL_TILE
    return pl.pallas_call(
        _pool_kernel,
        grid=(BATCH, nt),
        in_specs=[
            pl.BlockSpec((None, SEQ, D_MODEL), lambda bi, t: (bi, 0, 0)),
            pl.BlockSpec((POOL_GROUPS, POOL_GROUP_W, POOL_GROUP_W), lambda bi, t: (0, 0, 0)),
            pl.BlockSpec((1, D_MODEL), lambda bi, t: (0, 0)),
            pl.BlockSpec((1, D_MODEL), lambda bi, t: (0, 0)),
            pl.BlockSpec((1, D_MODEL), lambda bi, t: (0, 0)),
        ],
        out_specs=pl.BlockSpec((POOL_TILE, D_MODEL), lambda bi, t: (bi * nt + t, 0)),
        out_shape=jax.ShapeDtypeStruct((TOKENS, D_MODEL), F32),
        compiler_params=_params(("parallel", "arbitrary"), V7X_VMEM_LIMIT),
        name="pool_ln",
    )(x3, w_pool, scale, g, b)


ROUTER_TM = 512


def _router_kernel(x_ref, wt_ref, rb_ref, gates_ref):
    x_hi, x_lo = _split_bf16(x_ref[...])
    w_hi, w_lo = _split_bf16(wt_ref[...])
    logits = _dot_nt(w_hi, x_hi) + _dot_nt(w_hi, x_lo) + _dot_nt(w_lo, x_hi)
    aff = 1.0 / (1.0 + jnp.exp(-logits))
    sel = aff + rb_ref[...]
    a = [aff[e:e + 1, :] for e in range(N_EXPERTS)]
    s = [sel[e:e + 1, :] for e in range(N_EXPERTS)]
    one = jnp.ones_like(s[0])
    zero = jnp.zeros_like(s[0])
    rank = [zero] * N_EXPERTS
    for gi in range(N_EXPERT_GROUPS):
        for i in range(EXPERTS_PER_GROUP):
            for j in range(i + 1, EXPERTS_PER_GROUP):
                ei, ej = gi * EXPERTS_PER_GROUP + i, gi * EXPERTS_PER_GROUP + j
                j_wins = s[ej] > s[ei]
                rank[ei] = rank[ei] + jnp.where(j_wins, one, zero)
                rank[ej] = rank[ej] + jnp.where(j_wins, zero, one)
    top2 = [rank[e] < 2.0 for e in range(N_EXPERTS)]
    score = []
    for gi in range(N_EXPERT_GROUPS):
        acc = zero
        for i in range(EXPERTS_PER_GROUP):
            e = gi * EXPERTS_PER_GROUP + i
            acc = acc + jnp.where(top2[e], s[e], zero)
        score.append(acc)
    best, best_g = score[0], zero
    for gi in range(1, N_EXPERT_GROUPS):
        better = score[gi] > best
        best = jnp.where(better, score[gi], best)
        best_g = jnp.where(better, float(gi) * one, best_g)
    picked = [top2[e] & (best_g == float(e // EXPERTS_PER_GROUP)) for e in range(N_EXPERTS)]
    denom = zero
    for e in range(N_EXPERTS):
        denom = denom + jnp.where(picked[e], a[e], zero)
    gates_t = jnp.concatenate([jnp.where(picked[e], a[e] / denom, zero) for e in range(N_EXPERTS)], axis=0)
    gates_ref[...] = gates_t.T


def _router(x1, w_router_t, router_bias_col):
    tm = ROUTER_TM
    return pl.pallas_call(
        _router_kernel,
        grid=(TOKENS // tm,),
        in_specs=[
            pl.BlockSpec((tm, D_MODEL), lambda i: (i, 0)),
            pl.BlockSpec((N_EXPERTS, D_MODEL), lambda i: (0, 0)),
            pl.BlockSpec((N_EXPERTS, 1), lambda i: (0, 0)),
        ],
        out_specs=pl.BlockSpec((tm, N_EXPERTS), lambda i: (i, 0)),
        out_shape=jax.ShapeDtypeStruct((TOKENS, N_EXPERTS), F32),
        compiler_params=_params(("parallel",), V7X_VMEM_LIMIT),
        name="router",
    )(x1, w_router_t, router_bias_col)


MOE_TM = 1024


def _moe_kernel(x_ref, gates_ref, wg_ref, wu_ref, wd_ref, g_ref, b_ref, out_ref, acc_ref):
    e = pl.program_id(1)

    @pl.when(e == 0)
    def _():
        acc_ref[...] = jnp.zeros_like(acc_ref)

    xb = x_ref[...].astype(BF16)
    hg = _dot(xb, wg_ref[...])
    hu = _dot(xb, wu_ref[...])
    h = (hg / (1.0 + jnp.exp(-hg)) * hu).astype(BF16)
    y = _dot(h, wd_ref[...])
    lane = lax.broadcasted_iota(jnp.int32, gates_ref.shape, 1)
    gate = jnp.sum(jnp.where(lane == e, gates_ref[...], 0.0), axis=-1, keepdims=True)
    acc_ref[...] += gate * y

    @pl.when(e == N_EXPERTS - 1)
    def _():
        out_ref[...] = _layer_norm(ALPHA * x_ref[...] + acc_ref[...], g_ref[...], b_ref[...])


def _moe_ln(x1, gates, wg, wu, wd, g, b):
    tm = MOE_TM
    wspec = pl.BlockSpec((None, D_MODEL, D_MODEL), lambda i, e: (e, 0, 0))
    return pl.pallas_call(
        _moe_kernel,
        grid=(TOKENS // tm, N_EXPERTS),
        in_specs=[
            pl.BlockSpec((tm, D_MODEL), lambda i, e: (i, 0)),
            pl.BlockSpec((tm, N_EXPERTS), lambda i, e: (i, 0)),
            wspec, wspec, wspec,
            pl.BlockSpec((1, D_MODEL), lambda i, e: (0, 0)),
            pl.BlockSpec((1, D_MODEL), lambda i, e: (0, 0)),
        ],
        out_specs=pl.BlockSpec((tm, D_MODEL), lambda i, e: (i, 0)),
        out_shape=jax.ShapeDtypeStruct((TOKENS, D_MODEL), F32),
        scratch_shapes=[pltpu.VMEM((tm, D_MODEL), F32)],
        compiler_params=_params(("parallel", "arbitrary"), V7X_VMEM_LIMIT),
        name="moe_ln",
    )(x1, gates, wg, wu, wd, g, b)


def kernel(x, rel_bias, w_in, w_gla_gate, b_gla_gate, gla_norm, diff_lambda, diff_norm, w_out_mix, w_pool,
           pool_scale, ln_g, ln_b, w_router, router_bias, w_gate, w_up, w_down):
    x2 = x.reshape(TOKENS, D_MODEL)
    bias_tiles = _bias_tiles(rel_bias)
    w_router_t = w_router.T
    router_bias_col = router_bias.reshape(N_EXPERTS, 1)
    row = lambda v: v.reshape(1, -1)
    for layer in range(DEPTH):
        if layer % 2 == 0:
            i = layer // 2
            wi = w_in[i]
            w_main = jnp.concatenate([wi[:, :1536], wi[:, 1568:]], axis=1).astype(BF16)
            w_lr = jnp.pad(wi[:, 1536:1568], ((0, 0), (0, LR_COLS - 2 * GATE_RANK))).astype(BF16)
            hb, lr = _inproj(x2, w_main, w_lr)
            wg_pad = jnp.zeros((2, LR_COLS, GLA_KEY), F32)
            wg_pad = wg_pad.at[0, 0:GATE_RANK].set(w_gla_gate[i, 0])
            wg_pad = wg_pad.at[1, GATE_RANK:2 * GATE_RANK].set(w_gla_gate[i, 1])
            o_gla = _gla(hb, lr, wg_pad, b_gla_gate[i], row(gla_norm[i]))
            lam_init = 0.8 - 0.6 * math.exp(-0.3 * layer)
            o_diff = _diff(hb, bias_tiles, diff_lambda[i], row(diff_norm[i]), lam_init)
            x1 = _outproj_ln(o_gla, o_diff, w_out_mix[i].astype(BF16), x2, row(ln_g[layer, 0]), row(ln_b[layer, 0]))
        else:
            j = layer // 2
            x1 = _pool_ln(x2.reshape(BATCH, SEQ, D_MODEL), w_pool[j].astype(BF16), row(pool_scale[j]),
                          row(ln_g[layer, 0]), row(ln_b[layer, 0]))
        gates = _router(x1, w_router_t, router_bias_col)
        x2 = _moe_ln(x1, gates, w_gate[layer].astype(BF16), w_up[layer].astype(BF16), w_down[layer].astype(BF16),
                     row(ln_g[layer, 1]), row(ln_b[layer, 1]))
    return x2.reshape(BATCH, SEQ, D_MODEL)
```

```python
import functools
import math

import jax
import jax.numpy as jnp
from jax import lax
from jax.experimental import pallas as pl
from jax.experimental.pallas import tpu as pltpu
from jax.experimental.pallas import tpu_sc as plsc

F32 = jnp.float32
BF16 = jnp.bfloat16
U32 = jnp.uint32
I32 = jnp.int32

D_MODEL = 1024
BATCH = 8
SEQ = 2048
DEPTH = 4
TOKENS = BATCH * SEQ

GLA_HEADS = 4
GLA_WIDTH = 512
GLA_HEAD_V = 128
GLA_KEY = 256
GLA_HEAD_K = 64
GATE_RANK = 16
GATE_NORMALIZER = 16.0
GLA_CHUNK = 64
GLA_BLOCK = 256
CHUNKS_PER_BLOCK = GLA_BLOCK // GLA_CHUNK

DIFF_HEADS = 4
DIFF_WIDTH = 512
DIFF_HEAD_V = 128
DIFF_HEAD_DIM = 64
REL_BUCKETS = 32
Q_TILE = 256
KEY_BLOCK = 128
BIAS_SLOTS = 6

POOL_WINDOWS = (2, 4, 8, 16)
POOL_GROUPS = 4
POOL_GROUP_W = 256
POOL_TILE = 128
POOL_WIN = 3 * POOL_TILE

N_EXPERTS = 16
N_EXPERT_GROUPS = 4
EXPERTS_PER_GROUP = 4
TOP_K = 2

WORD_COLS = D_MODEL // 2
PIECE = 128
PIECES = WORD_COLS // PIECE
TOKEN_B = 128
EXPERT_TM = 512
PAIR_ROWS = TOKENS * TOP_K
SORTED_ROWS = PAIR_ROWS + N_EXPERTS * EXPERT_TM
N_TILES = SORTED_ROWS // EXPERT_TM
META_LANES = 128
assert N_TILES <= META_LANES

SC_CORES = 2
SC_SUBCORES = 16
SC_WORKERS = SC_CORES * SC_SUBCORES
SC_CHUNK = 128

ALPHA = (2.0 * DEPTH) ** 0.25
LN_EPS = 1e-5
RMS_EPS = 1e-6

H_COLS = 3072
LR_COLS = 128

V7X_VMEM_LIMIT = 56 * 1024 * 1024


def _params(sem, vmem=None):
    return pltpu.CompilerParams(dimension_semantics=sem, vmem_limit_bytes=vmem)


def _split_bf16(x):
    hi = x.astype(BF16)
    lo = (x - hi.astype(F32)).astype(BF16)
    return hi, lo


def _dot(a, b):
    return jnp.dot(a, b, preferred_element_type=F32)


def _dot_nt(a, b):
    return lax.dot_general(a, b, (((1,), (1,)), ((), ())), preferred_element_type=F32)


def _dot_tn(a, b):
    return lax.dot_general(a, b, (((0,), (0,)), ((), ())), preferred_element_type=F32)


def _layer_norm(z, g, b):
    mu = jnp.mean(z, axis=-1, keepdims=True)
    zc = z - mu
    var = jnp.mean(zc * zc, axis=-1, keepdims=True)
    return zc * lax.rsqrt(var + LN_EPS) * g + b


def _pack_words(x):
    lo = lax.bitcast_convert_type(x[:, :WORD_COLS].astype(BF16).astype(F32), U32)
    hi = lax.bitcast_convert_type(x[:, WORD_COLS:].astype(BF16).astype(F32), U32)
    return (lo >> 16) | (hi & jnp.uint32(0xFFFF0000))


def _unpack_words(w):
    lo = lax.bitcast_convert_type(w << 16, F32)
    hi = lax.bitcast_convert_type(w & jnp.uint32(0xFFFF0000), F32)
    return lo, hi


def _to_storage(w, b):
    out = []
    for s in range(w.shape[0] // b):
        for j in range(PIECES):
            out.append(w[s * b:(s + 1) * b, j * PIECE:(j + 1) * PIECE])
    return jnp.concatenate(out, axis=0)


def _from_storage(r, b):
    rows = []
    for s in range(r.shape[0] // (b * PIECES)):
        base = s * b * PIECES
        rows.append(jnp.concatenate([r[base + j * b:base + (j + 1) * b, :] for j in range(PIECES)], axis=1))
    return jnp.concatenate(rows, axis=0) if len(rows) > 1 else rows[0]


INPROJ_TM = 512


def _inproj_kernel(x_ref, w_ref, wlr_ref, hb_ref, lr_ref):
    xb = x_ref[...].astype(BF16)
    hb_ref[...] = _dot(xb, w_ref[...]).astype(BF16)
    lr_ref[...] = _dot(xb, wlr_ref[...])


def _inproj(x2, w_main, w_lr):
    return pl.pallas_call(
        _inproj_kernel,
        grid=(TOKENS // INPROJ_TM,),
        in_specs=[
            pl.BlockSpec((INPROJ_TM, D_MODEL), lambda i: (i, 0)),
            pl.BlockSpec((D_MODEL, H_COLS), lambda i: (0, 0)),
            pl.BlockSpec((D_MODEL, LR_COLS), lambda i: (0, 0)),
        ],
        out_specs=[
            pl.BlockSpec((INPROJ_TM, H_COLS), lambda i: (i, 0)),
            pl.BlockSpec((INPROJ_TM, LR_COLS), lambda i: (i, 0)),
        ],
        out_shape=[
            jax.ShapeDtypeStruct((TOKENS, H_COLS), BF16),
            jax.ShapeDtypeStruct((TOKENS, LR_COLS), F32),
        ],
        compiler_params=_params(("parallel",), V7X_VMEM_LIMIT),
        name="inproj",
    )(x2, w_main, w_lr)


def _gla_kernel(q_ref, k_ref, v_ref, g_ref, lr_ref, wg_ref, bg_ref, nw_ref,
                out_ref, oacc_ref, state_ref):
    blk, ch = GLA_BLOCK, GLA_CHUNK
    n_blocks = SEQ // blk
    row = lax.broadcasted_iota(jnp.int32, (blk, blk), 0)
    col = lax.broadcasted_iota(jnp.int32, (blk, blk), 1)
    same_chunk = (row // ch) == (col // ch)
    srow = lax.broadcasted_iota(jnp.int32, (GLA_KEY, GLA_WIDTH), 0) // GLA_HEAD_K
    scol = lax.broadcasted_iota(jnp.int32, (GLA_KEY, GLA_WIDTH), 1) // GLA_HEAD_V
    state_mask = srow == scol
    lane_head = lax.broadcasted_iota(jnp.int32, (blk, 2 * GLA_HEAD_K), 1) // GLA_HEAD_K
    ones_tok = jnp.ones((ch, GLA_HEAD_V), BF16)

    def block_step(r0, direction):
        fwd = direction == 0
        rows = pl.ds(r0, blk)
        lr_hi, lr_lo = _split_bf16(lr_ref[rows, :])
        w_hi, w_lo = _split_bf16(wg_ref[direction])
        logit = (_dot(lr_hi, w_hi) + _dot(lr_lo, w_hi) + _dot(lr_hi, w_lo)
                 + bg_ref[direction:direction + 1, :])
        la = (jnp.minimum(logit, 0.0) - jnp.log(1.0 + jnp.exp(-jnp.abs(logit)))) * (1.0 / GATE_NORMALIZER)
        la_hi, la_lo = _split_bf16(la)
        tri = jnp.where(same_chunk & ((col <= row) if fwd else (col >= row)), 1.0, 0.0).astype(BF16)
        b = _dot(tri, la_hi) + _dot(tri, la_lo)
        b3 = b.reshape(CHUNKS_PER_BLOCK, ch, GLA_KEY)
        mid = ch // 2 - 1 if fwd else ch // 2
        last = ch - 1 if fwd else 0
        b_mid = jnp.broadcast_to(b3[:, mid:mid + 1, :], b3.shape).reshape(blk, GLA_KEY)
        b_last = jnp.broadcast_to(b3[:, last:last + 1, :], b3.shape).reshape(blk, GLA_KEY)
        qf = q_ref[rows, :].astype(F32) * (GLA_HEAD_K ** -0.5)
        kf = k_ref[rows, :].astype(F32)
        vb = v_ref[rows, :]
        qd = (qf * jnp.exp(b - b_mid)).astype(BF16)
        kd = (kf * jnp.exp(b_mid - b)).astype(BF16)
        qe = (qf * jnp.exp(b)).astype(BF16)
        kl = (kf * jnp.exp(b_last - b)).astype(BF16)
        keep = same_chunk & ((col <= row) if fwd else (col > row))
        intra = []
        for h in range(GLA_HEADS):
            pair = slice((h // 2) * 128, (h // 2) * 128 + 128)
            qh = jnp.where(lane_head == (h % 2), qd[:, pair], jnp.zeros_like(qd[:, pair]))
            sc = _dot_nt(qh, kd[:, pair])
            p = jnp.where(keep, sc, 0.0).astype(BF16)
            intra.append(_dot(p, vb[:, h * GLA_HEAD_V:(h + 1) * GLA_HEAD_V]))
        o_intra = jnp.concatenate(intra, axis=1)
        inter = [None] * CHUNKS_PER_BLOCK
        order = range(CHUNKS_PER_BLOCK) if fwd else range(CHUNKS_PER_BLOCK - 1, -1, -1)
        for c in order:
            cs = slice(c * ch, (c + 1) * ch)
            state = state_ref[...]
            inter[c] = _dot(qe[cs, :], state.astype(BF16))
            d_state = _dot_tn(kl[cs, :], vb[cs, :])
            tot = _dot_tn(la_hi[cs, :], ones_tok) + _dot_tn(la_lo[cs, :], ones_tok)
            decay = jnp.exp(tot)
            decay = jnp.concatenate([decay] * GLA_HEADS, axis=1)
            state_ref[...] = state * decay + jnp.where(state_mask, d_state, 0.0)
        return o_intra + jnp.concatenate(inter, axis=0)

    state_ref[...] = jnp.zeros_like(state_ref)

    def fwd_body(i, carry):
        r0 = pl.multiple_of(i * blk, blk)
        oacc_ref[pl.ds(r0, blk), :] = block_step(r0, 0)
        return carry

    lax.fori_loop(0, n_blocks, fwd_body, 0)
    state_ref[...] = jnp.zeros_like(state_ref)

    def bwd_body(i, carry):
        r0 = pl.multiple_of((n_blocks - 1 - i) * blk, blk)
        o = oacc_ref[pl.ds(r0, blk), :] + block_step(r0, 1)
        gate = g_ref[pl.ds(r0, blk), :].astype(F32)
        gate = gate / (1.0 + jnp.exp(-gate))
        outs = []
        for h in range(GLA_HEADS):
            oh = o[:, h * GLA_HEAD_V:(h + 1) * GLA_HEAD_V]
            ms = jnp.mean(oh * oh, axis=-1, keepdims=True)
            outs.append(oh * lax.rsqrt(ms + RMS_EPS) * nw_ref[...])
        out_ref[pl.ds(r0, blk), :] = (jnp.concatenate(outs, axis=1) * gate).astype(BF16)
        return carry

    lax.fori_loop(0, n_blocks, bwd_body, 0)


def _gla(hb, lr, wg_pad, bg, nw):
    return pl.pallas_call(
        _gla_kernel,
        grid=(BATCH,),
        in_specs=[
            pl.BlockSpec((SEQ, GLA_KEY), lambda b: (b, 0)),
            pl.BlockSpec((SEQ, GLA_KEY), lambda b: (b, 1)),
            pl.BlockSpec((SEQ, GLA_WIDTH), lambda b: (b, 1)),
            pl.BlockSpec((SEQ, GLA_WIDTH), lambda b: (b, 2)),
            pl.BlockSpec((SEQ, LR_COLS), lambda b: (b, 0)),
            pl.BlockSpec((2, LR_COLS, GLA_KEY), lambda b: (0, 0, 0)),
            pl.BlockSpec((2, GLA_KEY), lambda b: (0, 0)),
            pl.BlockSpec((1, GLA_HEAD_V), lambda b: (0, 0)),
        ],
        out_specs=pl.BlockSpec((SEQ, GLA_WIDTH), lambda b: (b, 0)),
        out_shape=jax.ShapeDtypeStruct((TOKENS, GLA_WIDTH), BF16),
        scratch_shapes=[
            pltpu.VMEM((SEQ, GLA_WIDTH), F32),
            pltpu.VMEM((GLA_KEY, GLA_WIDTH), F32),
        ],
        compiler_params=_params(("parallel",), V7X_VMEM_LIMIT),
        name="gla",
    )(hb, hb, hb, hb, lr, wg_pad, bg, nw)


_BUCKET_STEPS = (12, 16, 23, 32, 46, 64, 91)


def _bias_kernel(rb_ref, out_ref):
    r = lax.broadcasted_iota(jnp.int32, (Q_TILE, KEY_BLOCK), 0)
    l = lax.broadcasted_iota(jnp.int32, (Q_TILE, KEY_BLOCK), 1)
    for s in range(BIAS_SLOTS):
        if s == 0:
            rel = jnp.full((Q_TILE, KEY_BLOCK), -SEQ, jnp.int32)
        elif s == BIAS_SLOTS - 1:
            rel = jnp.full((Q_TILE, KEY_BLOCK), SEQ, jnp.int32)
        else:
            rel = (s - 2) * KEY_BLOCK + l - r
        n = jnp.abs(rel)
        large = jnp.full_like(n, 8)
        for t in _BUCKET_STEPS:
            large = large + (n >= t).astype(jnp.int32)
        bucket = jnp.where(rel > 0, REL_BUCKETS // 2, 0) + jnp.where(n < 8, n, large)
        for h in range(DIFF_HEADS):
            acc = jnp.zeros((Q_TILE, KEY_BLOCK), F32)
            for bkt in range(REL_BUCKETS):
                acc = jnp.where(bucket == bkt, rb_ref[bkt, h], acc)
            out_ref[h, s] = acc


def _bias_tiles(rel_bias):
    return pl.pallas_call(
        _bias_kernel,
        in_specs=[pl.BlockSpec(memory_space=pltpu.SMEM)],
        out_specs=pl.BlockSpec(memory_space=pltpu.VMEM),
        out_shape=jax.ShapeDtypeStruct((DIFF_HEADS, BIAS_SLOTS, Q_TILE, KEY_BLOCK), F32),
        name="bias_tiles",
    )(rel_bias)


def _diff_kernel(lam_init, q_ref, k_ref, v_ref, bias_ref, lam_ref, nw_ref, out_ref):
    qi = pl.program_id(2)
    lam = lam_ref[...]
    lam_full = (jnp.exp(jnp.sum(lam[0:1] * lam[1:2], axis=-1, keepdims=True))
                - jnp.exp(jnp.sum(lam[2:3] * lam[3:4], axis=-1, keepdims=True)) + lam_init)
    q = q_ref[...]
    k = k_ref[...]
    lane = lax.broadcasted_iota(jnp.int32, q.shape, 1)
    qs = (q.astype(F32) * (DIFF_HEAD_DIM ** -0.5)).astype(BF16)
    zero = jnp.zeros_like(qs)
    blocks_per_q = Q_TILE // KEY_BLOCK
    bias = jnp.concatenate(
        [bias_ref[jnp.clip(kb - blocks_per_q * qi + 2, 0, BIAS_SLOTS - 1)] for kb in range(SEQ // KEY_BLOCK)],
        axis=1)
    probs = []
    for c in range(2):
        qc = jnp.where((lane // DIFF_HEAD_DIM) == c, qs, zero)
        s = _dot_nt(qc, k) + bias
        m = jnp.max(s, axis=-1, keepdims=True)
        e = jnp.exp(s - m)
        probs.append(e * (1.0 / jnp.sum(e, axis=-1, keepdims=True)))
    p = (probs[0] - lam_full * probs[1]).astype(BF16)
    o = _dot(p, v_ref[...])
    ms = jnp.mean(o * o, axis=-1, keepdims=True)
    out_ref[...] = (o * lax.rsqrt(ms + RMS_EPS) * nw_ref[...] * (1.0 - lam_init)).astype(BF16)


def _diff(hb, bias_tiles, lam, nw, lam_init):
    nq = SEQ // Q_TILE
    qcol, kcol, vcol = 1536 // 128, 2048 // 128, 2560 // 128
    return pl.pallas_call(
        functools.partial(_diff_kernel, lam_init),
        grid=(BATCH, DIFF_HEADS, nq),
        in_specs=[
            pl.BlockSpec((Q_TILE, DIFF_HEAD_V), lambda b, h, i: (b * nq + i, qcol + h)),
            pl.BlockSpec((SEQ, DIFF_HEAD_V), lambda b, h, i: (b, kcol + h)),
            pl.BlockSpec((SEQ, DIFF_HEAD_V), lambda b, h, i: (b, vcol + h)),
            pl.BlockSpec((None, BIAS_SLOTS, Q_TILE, KEY_BLOCK), lambda b, h, i: (h, 0, 0, 0)),
            pl.BlockSpec((4, DIFF_HEAD_DIM), lambda b, h, i: (0, 0)),
            pl.BlockSpec((1, DIFF_HEAD_V), lambda b, h, i: (0, 0)),
        ],
        out_specs=pl.BlockSpec((Q_TILE, DIFF_HEAD_V), lambda b, h, i: (b * nq + i, h)),
        out_shape=jax.ShapeDtypeStruct((TOKENS, DIFF_WIDTH), BF16),
        compiler_params=_params(("parallel", "parallel", "parallel"), V7X_VMEM_LIMIT),
        name="diff_attn",
    )(hb, hb, hb, bias_tiles, lam, nw)


OUTPROJ_TM = 512


def _outproj_kernel(og_ref, od_ref, w_ref, x_ref, g_ref, b_ref, out_ref, words_ref):
    mix = _dot(og_ref[...], w_ref[0:GLA_WIDTH, :]) + _dot(od_ref[...], w_ref[GLA_WIDTH:, :])
    x1 = _layer_norm(ALPHA * x_ref[...] + mix, g_ref[...], b_ref[...])
    out_ref[...] = x1
    words_ref[...] = _to_storage(_pack_words(x1), TOKEN_B)


def _outproj_ln(o_gla, o_diff, w_out, x2, g, b):
    tm = OUTPROJ_TM
    return pl.pallas_call(
        _outproj_kernel,
        grid=(TOKENS // tm,),
        in_specs=[
            pl.BlockSpec((tm, GLA_WIDTH), lambda i: (i, 0)),
            pl.BlockSpec((tm, DIFF_WIDTH), lambda i: (i, 0)),
            pl.BlockSpec((D_MODEL, D_MODEL), lambda i: (0, 0)),
            pl.BlockSpec((tm, D_MODEL), lambda i: (i, 0)),
            pl.BlockSpec((1, D_MODEL), lambda i: (0, 0)),
            pl.BlockSpec((1, D_MODEL), lambda i: (0, 0)),
        ],
        out_specs=[
            pl.BlockSpec((tm, D_MODEL), lambda i: (i, 0)),
            pl.BlockSpec((tm * PIECES, PIECE), lambda i: (i, 0)),
        ],
        out_shape=[
            jax.ShapeDtypeStruct((TOKENS, D_MODEL), F32),
            jax.ShapeDtypeStruct((TOKENS * PIECES, PIECE), U32),
        ],
        compiler_params=_params(("parallel",), V7X_VMEM_LIMIT),
        name="outproj_ln",
    )(o_gla, o_diff, w_out, x2, g, b)


def _pool_kernel(x_ref, w_ref, sc_ref, g_ref, b_ref, out_ref, words_ref):
    t = pl.program_id(1)
    r0 = pl.multiple_of(t * POOL_TILE, POOL_TILE)
    w0 = pl.multiple_of(jnp.clip(r0 - POOL_TILE, 0, SEQ - POOL_WIN), POOL_TILE)
    xt = x_ref[pl.ds(r0, POOL_TILE), :]
    xw_hi, xw_lo = _split_bf16(x_ref[pl.ds(w0, POOL_WIN), :])
    pos = r0 + lax.broadcasted_iota(jnp.int32, (POOL_TILE, POOL_WIN), 0)
    key = w0 + lax.broadcasted_iota(jnp.int32, (POOL_TILE, POOL_WIN), 1)
    posc = r0 + lax.broadcasted_iota(jnp.int32, (POOL_TILE, 1), 0)
    ys = []
    for gi, win in enumerate(POOL_WINDOWS):
        half = win // 2
        band = jnp.where((key >= pos - half) & (key < pos + half), 1.0, 0.0).astype(BF16)
        cols = slice(gi * POOL_GROUP_W, (gi + 1) * POOL_GROUP_W)
        wsum = _dot(band, xw_hi[:, cols]) + _dot(band, xw_lo[:, cols])
        count = (jnp.minimum(posc + half, SEQ) - jnp.maximum(posc - half, 0)).astype(F32)
        pooled = wsum / count - xt[:, cols]
        ys.append(_dot(pooled.astype(BF16), w_ref[gi]))
    mix = jnp.concatenate(ys, axis=1) * sc_ref[...]
    x1 = _layer_norm(ALPHA * xt + mix, g_ref[...], b_ref[...])
    out_ref[...] = x1
    words_ref[...] = _to_storage(_pack_words(x1), TOKEN_B)


def _pool_ln(x3, w_pool, scale, g, b):
    nt = SEQ // POOL_TILE
    return pl.pallas_call(
        _pool_kernel,
        grid=(BATCH, nt),
        in_specs=[
            pl.BlockSpec((None, SEQ, D_MODEL), lambda bi, t: (bi, 0, 0)),
            pl.BlockSpec((POOL_GROUPS, POOL_GROUP_W, POOL_GROUP_W), lambda bi, t: (0, 0, 0)),
            pl.BlockSpec((1, D_MODEL), lambda bi, t: (0, 0)),
            pl.BlockSpec((1, D_MODEL), lambda bi, t: (0, 0)),
            pl.BlockSpec((1, D_MODEL), lambda bi, t: (0, 0)),
        ],
        out_specs=[
            pl.BlockSpec((POOL_TILE, D_MODEL), lambda bi, t: (bi * nt + t, 0)),
            pl.BlockSpec((POOL_TILE * PIECES, PIECE), lambda bi, t: (bi * nt + t, 0)),
        ],
        out_shape=[
            jax.ShapeDtypeStruct((TOKENS, D_MODEL), F32),
            jax.ShapeDtypeStruct((TOKENS * PIECES, PIECE), U32),
        ],
        compiler_params=_params(("parallel", "arbitrary"), V7X_VMEM_LIMIT),
        name="pool_ln",
    )(x3, w_pool, scale, g, b)


ROUTER_TM = 512


def _router_kernel(x_ref, wt_ref, rb_ref, eid_ref, wcol_ref):
    x_hi, x_lo = _split_bf16(x_ref[...])
    w_hi, w_lo = _split_bf16(wt_ref[...])
    logits = _dot_nt(w_hi, x_hi) + _dot_nt(w_hi, x_lo) + _dot_nt(w_lo, x_hi)
    aff = 1.0 / (1.0 + jnp.exp(-logits))
    sel = aff + rb_ref[...]
    a = [aff[e:e + 1, :] for e in range(N_EXPERTS)]
    s = [sel[e:e + 1, :] for e in range(N_EXPERTS)]
    one = jnp.ones_like(s[0])
    zero = jnp.zeros_like(s[0])
    rank = [zero] * N_EXPERTS
    for gi in range(N_EXPERT_GROUPS):
        for i in range(EXPERTS_PER_GROUP):
            for j in range(i + 1, EXPERTS_PER_GROUP):
                ei, ej = gi * EXPERTS_PER_GROUP + i, gi * EXPERTS_PER_GROUP + j
                j_wins = s[ej] > s[ei]
                rank[ei] = rank[ei] + jnp.where(j_wins, one, zero)
                rank[ej] = rank[ej] + jnp.where(j_wins, zero, one)
    top2 = [rank[e] < 2.0 for e in range(N_EXPERTS)]
    score = []
    for gi in range(N_EXPERT_GROUPS):
        acc = zero
        for i in range(EXPERTS_PER_GROUP):
            e = gi * EXPERTS_PER_GROUP + i
            acc = acc + jnp.where(top2[e], s[e], zero)
        score.append(acc)
    best, best_g = score[0], zero
    for gi in range(1, N_EXPERT_GROUPS):
        better = score[gi] > best
        best = jnp.where(better, score[gi], best)
        best_g = jnp.where(better, float(gi) * one, best_g)
    picked = [top2[e] & (best_g == float(e // EXPERTS_PER_GROUP)) for e in range(N_EXPERTS)]
    e_lo, e_hi = 99.0 * one, -one
    for e in range(N_EXPERTS):
        e_lo = jnp.where(picked[e], jnp.minimum(e_lo, float(e)), e_lo)
        e_hi = jnp.where(picked[e], jnp.maximum(e_hi, float(e)), e_hi)
    a_lo, a_hi = zero, zero
    for e in range(N_EXPERTS):
        a_lo = jnp.where(picked[e] & (e_lo == float(e)), a[e], a_lo)
        a_hi = jnp.where(picked[e] & (e_hi == float(e)), a[e], a_hi)
    denom = a_lo + a_hi
    eid_ref[...] = jnp.concatenate([e_lo, e_hi], axis=0).astype(I32)
    w8 = jnp.concatenate([a_lo / denom, a_hi / denom] + [zero] * 6, axis=0)
    wcol_ref[...] = w8.T


def _router(x1, w_router_t, router_bias_col):
    tm = ROUTER_TM
    return pl.pallas_call(
        _router_kernel,
        grid=(TOKENS // tm,),
        in_specs=[
            pl.BlockSpec((tm, D_MODEL), lambda i: (i, 0)),
            pl.BlockSpec((N_EXPERTS, D_MODEL), lambda i: (0, 0)),
            pl.BlockSpec((N_EXPERTS, 1), lambda i: (0, 0)),
        ],
        out_specs=[
            pl.BlockSpec((TOP_K, tm), lambda i: (0, i)),
            pl.BlockSpec((tm, 8), lambda i: (i, 0)),
        ],
        out_shape=[
            jax.ShapeDtypeStruct((TOP_K, TOKENS), I32),
            jax.ShapeDtypeStruct((TOKENS, 8), F32),
        ],
        compiler_params=_params(("parallel",), V7X_VMEM_LIMIT),
        name="router",
    )(x1, w_router_t, router_bias_col)


PLAN_CHUNK = 512


def _plan_kernel(eid_ref, pos_ref, meta_ref):
    n_chunks = TOKENS // PLAN_CHUNK
    erow = lax.broadcasted_iota(I32, (N_EXPERTS, PLAN_CHUNK), 0)

    def lanes(c):
        return pl.ds(pl.multiple_of(c * PLAN_CHUNK, PLAN_CHUNK), PLAN_CHUNK)

    def onehot(k, c):
        return erow == eid_ref[k:k + 1, lanes(c)]

    counts = jnp.zeros((N_EXPERTS, 1), F32)
    for k in range(TOP_K):
        counts = lax.fori_loop(
            0, n_chunks,
            lambda c, cnt, k=k: cnt + jnp.sum(jnp.where(onehot(k, c), 1.0, 0.0), axis=1, keepdims=True),
            counts)
    padded = jnp.ceil(counts * (1.0 / EXPERT_TM)) * EXPERT_TM
    ei = lax.broadcasted_iota(I32, (N_EXPERTS, N_EXPERTS), 0)
    ej = lax.broadcasted_iota(I32, (N_EXPERTS, N_EXPERTS), 1)
    padded_row = jnp.sum(jnp.where(ei == ej, padded, 0.0), axis=0, keepdims=True)
    start = jnp.sum(jnp.where(ej < ei, padded_row, 0.0), axis=1, keepdims=True)
    end = start + padded
    total = jnp.sum(padded, axis=0, keepdims=True)

    tri = jnp.where(lax.broadcasted_iota(I32, (PLAN_CHUNK, PLAN_CHUNK), 0)
                    <= lax.broadcasted_iota(I32, (PLAN_CHUNK, PLAN_CHUNK), 1), 1.0, 0.0).astype(BF16)

    def pos_body(c, carry, k):
        oh = onehot(k, c)
        ohf = jnp.where(oh, 1.0, 0.0)
        prefix = _dot(ohf.astype(BF16), tri)
        pos = jnp.sum(jnp.where(oh, start + carry + prefix - 1.0, 0.0), axis=0, keepdims=True)
        pos_ref[k:k + 1, lanes(c)] = pos.astype(I32)
        return carry + jnp.sum(ohf, axis=1, keepdims=True)

    carry = jnp.zeros((N_EXPERTS, 1), F32)
    for k in range(TOP_K):
        carry = lax.fori_loop(0, n_chunks, functools.partial(pos_body, k=k), carry)

    tile0 = lax.broadcasted_iota(I32, (1, META_LANES), 1).astype(F32) * EXPERT_TM
    n_used = total * (1.0 / EXPERT_TM)
    last = n_used - 1.0
    tile_c = jnp.minimum(tile0, last * EXPERT_TM)
    expert = jnp.sum(jnp.where(end <= tile_c, 1.0, 0.0), axis=0, keepdims=True)
    block = tile_c * (1.0 / EXPERT_TM)
    meta = jnp.concatenate([expert, block, jnp.broadcast_to(n_used, (1, META_LANES))]
                           + [jnp.zeros((1, META_LANES), F32)] * 5, axis=0)
    meta_ref[...] = meta.astype(I32)


def _plan(eid):
    return pl.pallas_call(
        _plan_kernel,
        in_specs=[pl.BlockSpec(memory_space=pltpu.VMEM)],
        out_specs=[pl.BlockSpec(memory_space=pltpu.VMEM), pl.BlockSpec(memory_space=pltpu.VMEM)],
        out_shape=[
            jax.ShapeDtypeStruct((TOP_K, TOKENS), I32),
            jax.ShapeDtypeStruct((8, META_LANES), I32),
        ],
        name="plan",
    )(eid)


def _sc_mesh():
    return plsc.VectorSubcoreMesh(core_axis_name="c", subcore_axis_name="s")


def _sc_worker_id():
    return lax.axis_index("s") * SC_CORES + lax.axis_index("c")


TOKEN_CHUNKS = TOKENS * PIECES // SC_CHUNK
CHUNKS_PER_WORKER = TOKEN_CHUNKS // SC_WORKERS


def _dispatch(words, didx):
    @functools.partial(
        pl.kernel,
        out_type=jax.ShapeDtypeStruct((SORTED_ROWS * PIECES, PIECE), U32),
        mesh=_sc_mesh(),
        scratch_types=[
            pltpu.VMEM((TOP_K, CHUNKS_PER_WORKER, SC_CHUNK), I32),
            pltpu.VMEM((SC_CHUNK, PIECE), U32),
        ],
        name="moe_dispatch",
    )
    def run(words_hbm, didx_hbm, out_hbm, idx_v, buf):
        c0 = _sc_worker_id() * CHUNKS_PER_WORKER
        for k in range(TOP_K):
            pltpu.sync_copy(didx_hbm.at[k, pl.ds(c0, CHUNKS_PER_WORKER)], idx_v.at[k])

        @pl.loop(0, CHUNKS_PER_WORKER)
        def _(c):
            pltpu.sync_copy(words_hbm.at[pl.ds((c0 + c) * SC_CHUNK, SC_CHUNK)], buf)
            for k in range(TOP_K):
                pltpu.sync_copy(buf, out_hbm.at[idx_v.at[k, c]])

    return run(words, didx)


def _combine(sorted_out, didx):
    @functools.partial(
        pl.kernel,
        out_type=jax.ShapeDtypeStruct((TOP_K, TOKENS * PIECES, PIECE), U32),
        mesh=_sc_mesh(),
        scratch_types=[
            pltpu.VMEM((TOP_K, CHUNKS_PER_WORKER, SC_CHUNK), I32),
            pltpu.VMEM((SC_CHUNK, PIECE), U32),
        ],
        name="moe_combine",
    )
    def run(src_hbm, didx_hbm, out_hbm, idx_v, buf):
        c0 = _sc_worker_id() * CHUNKS_PER_WORKER
        for k in range(TOP_K):
            pltpu.sync_copy(didx_hbm.at[k, pl.ds(c0, CHUNKS_PER_WORKER)], idx_v.at[k])

        @pl.loop(0, CHUNKS_PER_WORKER)
        def _(c):
            for k in range(TOP_K):
                pltpu.sync_copy(src_hbm.at[idx_v.at[k, c]], buf)
                pltpu.sync_copy(buf, out_hbm.at[k, pl.ds((c0 + c) * SC_CHUNK, SC_CHUNK)])

    return run(sorted_out, didx)


def _expert_kernel(te_ref, tb_ref, nu_ref, xs_ref, wg_ref, wu_ref, wd_ref, out_ref, wgb, wub, wdb):
    i = pl.program_id(0)
    fresh = (i == 0) | (te_ref[i] != te_ref[jnp.maximum(i - 1, 0)])

    @pl.when(fresh)
    def _():
        wgb[...] = wg_ref[...].astype(BF16)
        wub[...] = wu_ref[...].astype(BF16)
        wdb[...] = wd_ref[...].astype(BF16)

    @pl.when(i < nu_ref[0])
    def _():
        lo, hi = _unpack_words(_from_storage(xs_ref[...], EXPERT_TM))
        lo, hi = lo.astype(BF16), hi.astype(BF16)
        hg = _dot(lo, wgb[0:WORD_COLS, :]) + _dot(hi, wgb[WORD_COLS:, :])
        hu = _dot(lo, wub[0:WORD_COLS, :]) + _dot(hi, wub[WORD_COLS:, :])
        h = (hg / (1.0 + jnp.exp(-hg)) * hu).astype(BF16)
        y = _dot(h, wdb[...])
        out_ref[...] = _to_storage(_pack_words(y), EXPERT_TM)


def _experts(te, tb, nu, xs, w_gate, w_up, w_down, layer):
    wspec = pl.BlockSpec((None, None, D_MODEL, D_MODEL), lambda i, te, tb, nu: (layer, te[i], 0, 0))
    rows = pl.BlockSpec((EXPERT_TM * PIECES, PIECE), lambda i, te, tb, nu: (tb[i], 0))
    return pl.pallas_call(
        _expert_kernel,
        grid_spec=pltpu.PrefetchScalarGridSpec(
            num_scalar_prefetch=3,
            grid=(N_TILES,),
            in_specs=[rows, wspec, wspec, wspec],
            out_specs=rows,
            scratch_shapes=[pltpu.VMEM((D_MODEL, D_MODEL), BF16)] * 3,
        ),
        out_shape=jax.ShapeDtypeStruct((SORTED_ROWS * PIECES, PIECE), U32),
        compiler_params=_params(("arbitrary",), V7X_VMEM_LIMIT),
        name="experts",
    )(te, tb, nu, xs, w_gate, w_up, w_down)


COMBINE_TM = 512


def _combine_ln_kernel(x_ref, y_ref, w_ref, g_ref, b_ref, out_ref):
    w = w_ref[...]
    ffn = None
    for k in range(TOP_K):
        lo, hi = _unpack_words(_from_storage(y_ref[k], TOKEN_B))
        yk = w[:, k:k + 1] * jnp.concatenate([lo, hi], axis=1)
        ffn = yk if ffn is None else ffn + yk
    out_ref[...] = _layer_norm(ALPHA * x_ref[...] + ffn, g_ref[...], b_ref[...])


def _combine_ln(x1, y_tok, wcol, g, b):
    tm = COMBINE_TM
    return pl.pallas_call(
        _combine_ln_kernel,
        grid=(TOKENS // tm,),
        in_specs=[
            pl.BlockSpec((tm, D_MODEL), lambda i: (i, 0)),
            pl.BlockSpec((TOP_K, tm * PIECES, PIECE), lambda i: (0, i, 0)),
            pl.BlockSpec((tm, 8), lambda i: (i, 0)),
            pl.BlockSpec((1, D_MODEL), lambda i: (0, 0)),
            pl.BlockSpec((1, D_MODEL), lambda i: (0, 0)),
        ],
        out_specs=pl.BlockSpec((tm, D_MODEL), lambda i: (i, 0)),
        out_shape=jax.ShapeDtypeStruct((TOKENS, D_MODEL), F32),
        compiler_params=_params(("parallel",), V7X_VMEM_LIMIT),
        name="combine_ln",
    )(x1, y_tok, wcol, g, b)


def _sorted_storage_rows(pos):
    p = pos.reshape(TOP_K, TOKENS // TOKEN_B, 1, TOKEN_B)
    j = jnp.arange(PIECES, dtype=I32).reshape(1, 1, PIECES, 1)
    r = (p // EXPERT_TM) * (PIECES * EXPERT_TM) + j * EXPERT_TM + p % EXPERT_TM
    return r.reshape(TOP_K, TOKEN_CHUNKS, SC_CHUNK)


def kernel(x, rel_bias, w_in, w_gla_gate, b_gla_gate, gla_norm, diff_lambda, diff_norm, w_out_mix, w_pool,
           pool_scale, ln_g, ln_b, w_router, router_bias, w_gate, w_up, w_down):
    x2 = x.reshape(TOKENS, D_MODEL)
    bias_tiles = _bias_tiles(rel_bias)
    w_router_t = w_router.T
    router_bias_col = router_bias.reshape(N_EXPERTS, 1)
    row = lambda v: v.reshape(1, -1)
    for layer in range(DEPTH):
        if layer % 2 == 0:
            i = layer // 2
            wi = w_in[i]
            w_main = jnp.concatenate([wi[:, :1536], wi[:, 1568:]], axis=1).astype(BF16)
            w_lr = jnp.pad(wi[:, 1536:1568], ((0, 0), (0, LR_COLS - 2 * GATE_RANK))).astype(BF16)
            hb, lr = _inproj(x2, w_main, w_lr)
            wg_pad = jnp.zeros((2, LR_COLS, GLA_KEY), F32)
            wg_pad = wg_pad.at[0, 0:GATE_RANK].set(w_gla_gate[i, 0])
            wg_pad = wg_pad.at[1, GATE_RANK:2 * GATE_RANK].set(w_gla_gate[i, 1])
            o_gla = _gla(hb, lr, wg_pad, b_gla_gate[i], row(gla_norm[i]))
            lam_init = 0.8 - 0.6 * math.exp(-0.3 * layer)
            o_diff = _diff(hb, bias_tiles, diff_lambda[i], row(diff_norm[i]), lam_init)
            x1, words = _outproj_ln(o_gla, o_diff, w_out_mix[i].astype(BF16), x2,
                                    row(ln_g[layer, 0]), row(ln_b[layer, 0]))
        else:
            j = layer // 2
            x1, words = _pool_ln(x2.reshape(BATCH, SEQ, D_MODEL), w_pool[j].astype(BF16), row(pool_scale[j]),
                                 row(ln_g[layer, 0]), row(ln_b[layer, 0]))
        eid, wcol = _router(x1, w_router_t, router_bias_col)
        pos, meta = _plan(eid)
        didx = _sorted_storage_rows(pos)
        xs = _dispatch(words, didx)
        ys = _experts(meta[0, :N_TILES], meta[1, :N_TILES], meta[2, :1], xs, w_gate, w_up, w_down, layer)
        y_tok = _combine(ys, didx)
        x2 = _combine_ln(x1, y_tok, wcol, row(ln_g[layer, 1]), row(ln_b[layer, 1]))
    return x2.reshape(BATCH, SEQ, D_MODEL)
```

```python
import functools
import math

import jax
import jax.numpy as jnp
from jax import lax
from jax.experimental import pallas as pl
from jax.experimental.pallas import tpu as pltpu
from jax.experimental.pallas import tpu_sc as plsc

F32 = jnp.float32
BF16 = jnp.bfloat16
U32 = jnp.uint32
I32 = jnp.int32

D_MODEL = 1024
BATCH = 8
SEQ = 2048
DEPTH = 4
TOKENS = BATCH * SEQ

GLA_HEADS = 4
GLA_WIDTH = 512
GLA_HEAD_V = 128
GLA_KEY = 256
GLA_HEAD_K = 64
GATE_RANK = 16
GATE_NORMALIZER = 16.0
GLA_CHUNK = 64
GLA_BLOCK = 256
CHUNKS_PER_BLOCK = GLA_BLOCK // GLA_CHUNK

DIFF_HEADS = 4
DIFF_WIDTH = 512
DIFF_HEAD_V = 128
DIFF_HEAD_DIM = 64
REL_BUCKETS = 32
Q_TILE = 512
Q_SUB = 256
KEY_BLOCK = 128
BIAS_SLOTS = 6

POOL_WINDOWS = (2, 4, 8, 16)
POOL_GROUPS = 4
POOL_GROUP_W = 256
POOL_TILE = 256
POOL_HALO = 128
POOL_WIN = POOL_TILE + 2 * POOL_HALO

N_EXPERTS = 16
N_EXPERT_GROUPS = 4
EXPERTS_PER_GROUP = 4
TOP_K = 2

WORD_COLS = D_MODEL // 2
PIECE = 128
PIECES = WORD_COLS // PIECE
TOKEN_B = 128
EXPERT_TM = 512
PAIR_ROWS = TOKENS * TOP_K
SORTED_ROWS = PAIR_ROWS + N_EXPERTS * EXPERT_TM
N_TILES = SORTED_ROWS // EXPERT_TM
META_LANES = 128
assert N_TILES <= META_LANES

SC_CORES = 2
SC_SUBCORES = 16
SC_WORKERS = SC_CORES * SC_SUBCORES
SC_CHUNK = 128

LOG2E = math.log2(math.e)
ALPHA = (2.0 * DEPTH) ** 0.25
LN_EPS = 1e-5
RMS_EPS = 1e-6

H_COLS = 3072
LR_COLS = 128

V7X_VMEM_LIMIT = 56 * 1024 * 1024


def _params(sem, vmem=None):
    return pltpu.CompilerParams(dimension_semantics=sem, vmem_limit_bytes=vmem)


def _split_bf16(x):
    hi = x.astype(BF16)
    lo = (x - hi.astype(F32)).astype(BF16)
    return hi, lo


def _dot(a, b):
    return jnp.dot(a, b, preferred_element_type=F32)


def _dot_nt(a, b):
    return lax.dot_general(a, b, (((1,), (1,)), ((), ())), preferred_element_type=F32)


def _dot_tn(a, b):
    return lax.dot_general(a, b, (((0,), (0,)), ((), ())), preferred_element_type=F32)


def _layer_norm(z, g, b):
    mu = jnp.mean(z, axis=-1, keepdims=True)
    zc = z - mu
    var = jnp.mean(zc * zc, axis=-1, keepdims=True)
    return zc * lax.rsqrt(var + LN_EPS) * g + b


def _pack_words(x):
    lo = lax.bitcast_convert_type(x[:, :WORD_COLS].astype(BF16).astype(F32), U32)
    hi = lax.bitcast_convert_type(x[:, WORD_COLS:].astype(BF16).astype(F32), U32)
    return (lo >> 16) | (hi & jnp.uint32(0xFFFF0000))


def _unpack_words(w):
    lo = lax.bitcast_convert_type(w << 16, F32)
    hi = lax.bitcast_convert_type(w & jnp.uint32(0xFFFF0000), F32)
    return lo, hi


def _to_storage(w, b):
    out = []
    for s in range(w.shape[0] // b):
        for j in range(PIECES):
            out.append(w[s * b:(s + 1) * b, j * PIECE:(j + 1) * PIECE])
    return jnp.concatenate(out, axis=0)


def _from_storage(r, b):
    rows = []
    for s in range(r.shape[0] // (b * PIECES)):
        base = s * b * PIECES
        rows.append(jnp.concatenate([r[base + j * b:base + (j + 1) * b, :] for j in range(PIECES)], axis=1))
    return jnp.concatenate(rows, axis=0) if len(rows) > 1 else rows[0]


INPROJ_TM = 512


def _inproj_kernel(x_ref, w_ref, wlr_ref, hb_ref, lr_ref):
    xb = x_ref[...].astype(BF16)
    hb_ref[...] = _dot(xb, w_ref[...]).astype(BF16)
    lr_ref[...] = _dot(xb, wlr_ref[...])


def _inproj(x2, w_main, w_lr):
    return pl.pallas_call(
        _inproj_kernel,
        grid=(TOKENS // INPROJ_TM,),
        in_specs=[
            pl.BlockSpec((INPROJ_TM, D_MODEL), lambda i: (i, 0)),
            pl.BlockSpec((D_MODEL, H_COLS), lambda i: (0, 0)),
            pl.BlockSpec((D_MODEL, LR_COLS), lambda i: (0, 0)),
        ],
        out_specs=[
            pl.BlockSpec((INPROJ_TM, H_COLS), lambda i: (i, 0)),
            pl.BlockSpec((INPROJ_TM, LR_COLS), lambda i: (i, 0)),
        ],
        out_shape=[
            jax.ShapeDtypeStruct((TOKENS, H_COLS), BF16),
            jax.ShapeDtypeStruct((TOKENS, LR_COLS), F32),
        ],
        compiler_params=_params(("parallel",), V7X_VMEM_LIMIT),
        name="inproj",
    )(x2, w_main, w_lr)


def _gla_kernel(q_ref, k_ref, v_ref, g_ref, lr_ref, wg_ref, bg_ref, nw_ref,
                out_ref, of_ref, ob_ref, sf_ref, sb_ref):
    blk, ch = GLA_BLOCK, GLA_CHUNK
    n_blocks = SEQ // blk
    row = lax.broadcasted_iota(jnp.int32, (blk, blk), 0)
    col = lax.broadcasted_iota(jnp.int32, (blk, blk), 1)
    same_chunk = (row // ch) == (col // ch)
    srow = lax.broadcasted_iota(jnp.int32, (GLA_WIDTH, GLA_KEY), 0) // GLA_HEAD_V
    scol = lax.broadcasted_iota(jnp.int32, (GLA_WIDTH, GLA_KEY), 1) // GLA_HEAD_K
    state_mask = srow == scol
    lane_head = lax.broadcasted_iota(jnp.int32, (blk, 2 * GLA_HEAD_K), 1) // GLA_HEAD_K

    def block_step(r0, direction, state_ref):
        fwd = direction == 0
        rows = pl.ds(r0, blk)
        lr_hi, lr_lo = _split_bf16(lr_ref[rows, :])
        w_hi, w_lo = _split_bf16(wg_ref[direction])
        logit = (_dot(lr_hi, w_hi) + _dot(lr_lo, w_hi) + _dot(lr_hi, w_lo)
                 + bg_ref[direction:direction + 1, :])
        la = (jnp.minimum(logit, 0.0) - jnp.log(1.0 + jnp.exp(-jnp.abs(logit)))) * (1.0 / GATE_NORMALIZER)
        la_hi, la_lo = _split_bf16(la)
        tri = jnp.where(same_chunk & ((col <= row) if fwd else (col >= row)), 1.0, 0.0).astype(BF16)
        b = _dot(tri, la_hi) + _dot(tri, la_lo)
        b3 = b.reshape(CHUNKS_PER_BLOCK, ch, GLA_KEY)
        mid = ch // 2 - 1 if fwd else ch // 2
        last = ch - 1 if fwd else 0
        b_mid = jnp.broadcast_to(b3[:, mid:mid + 1, :], b3.shape).reshape(blk, GLA_KEY)
        b_last = jnp.broadcast_to(b3[:, last:last + 1, :], b3.shape).reshape(blk, GLA_KEY)
        qf = q_ref[rows, :].astype(F32)
        kf = k_ref[rows, :].astype(F32)
        vb = v_ref[rows, :]
        qd = (qf * jnp.exp(b - b_mid)).astype(BF16)
        kd = (kf * jnp.exp(b_mid - b)).astype(BF16)
        qe = (qf * jnp.exp(b)).astype(BF16)
        kl = (kf * jnp.exp(b_last - b)).astype(BF16)
        keep = same_chunk & ((col <= row) if fwd else (col > row))
        intra = []
        for h in range(GLA_HEADS):
            pair = slice((h // 2) * 128, (h // 2) * 128 + 128)
            qh = jnp.where(lane_head == (h % 2), qd[:, pair], jnp.zeros_like(qd[:, pair]))
            sc = _dot_nt(qh, kd[:, pair])
            p = jnp.where(keep, sc, 0.0).astype(BF16)
            intra.append(_dot(p, vb[:, h * GLA_HEAD_V:(h + 1) * GLA_HEAD_V]))
        o_intra = jnp.concatenate(intra, axis=1)
        inter = [None] * CHUNKS_PER_BLOCK
        order = range(CHUNKS_PER_BLOCK) if fwd else range(CHUNKS_PER_BLOCK - 1, -1, -1)
        for c in order:
            cs = slice(c * ch, (c + 1) * ch)
            state = state_ref[...]
            inter[c] = _dot_nt(qe[cs, :], state.astype(BF16))
            d_state = _dot_tn(vb[cs, :], kl[cs, :])
            decay = jnp.exp(b3[c, last:last + 1, :])
            state_ref[...] = state * decay + jnp.where(state_mask, d_state, 0.0)
        return o_intra + jnp.concatenate(inter, axis=0)

    sf_ref[...] = jnp.zeros_like(sf_ref)
    sb_ref[...] = jnp.zeros_like(sb_ref)

    def scan_body(i, carry):
        rf = pl.multiple_of(i * blk, blk)
        rb = pl.multiple_of((n_blocks - 1 - i) * blk, blk)
        of_ref[pl.ds(rf, blk), :] = block_step(rf, 0, sf_ref)
        ob_ref[pl.ds(rb, blk), :] = block_step(rb, 1, sb_ref)
        return carry

    lax.fori_loop(0, n_blocks, scan_body, 0)

    def finish_body(i, carry):
        r0 = pl.multiple_of(i * blk, blk)
        o = of_ref[pl.ds(r0, blk), :] + ob_ref[pl.ds(r0, blk), :]
        gate = g_ref[pl.ds(r0, blk), :].astype(F32)
        gate = gate / (1.0 + jnp.exp(-gate))
        outs = []
        for h in range(GLA_HEADS):
            oh = o[:, h * GLA_HEAD_V:(h + 1) * GLA_HEAD_V]
            ms = jnp.mean(oh * oh, axis=-1, keepdims=True)
            outs.append(oh * lax.rsqrt(ms + RMS_EPS) * nw_ref[...])
        out_ref[pl.ds(r0, blk), :] = (jnp.concatenate(outs, axis=1) * gate).astype(BF16)
        return carry

    lax.fori_loop(0, n_blocks, finish_body, 0)


def _gla(hb, lr, wg_pad, bg, nw):
    return pl.pallas_call(
        _gla_kernel,
        grid=(BATCH,),
        in_specs=[
            pl.BlockSpec((SEQ, GLA_KEY), lambda b: (b, 0)),
            pl.BlockSpec((SEQ, GLA_KEY), lambda b: (b, 1)),
            pl.BlockSpec((SEQ, GLA_WIDTH), lambda b: (b, 1)),
            pl.BlockSpec((SEQ, GLA_WIDTH), lambda b: (b, 2)),
            pl.BlockSpec((SEQ, LR_COLS), lambda b: (b, 0)),
            pl.BlockSpec((2, LR_COLS, GLA_KEY), lambda b: (0, 0, 0)),
            pl.BlockSpec((2, GLA_KEY), lambda b: (0, 0)),
            pl.BlockSpec((1, GLA_HEAD_V), lambda b: (0, 0)),
        ],
        out_specs=pl.BlockSpec((SEQ, GLA_WIDTH), lambda b: (b, 0)),
        out_shape=jax.ShapeDtypeStruct((TOKENS, GLA_WIDTH), BF16),
        scratch_shapes=[
            pltpu.VMEM((SEQ, GLA_WIDTH), F32),
            pltpu.VMEM((SEQ, GLA_WIDTH), F32),
            pltpu.VMEM((GLA_WIDTH, GLA_KEY), F32),
            pltpu.VMEM((GLA_WIDTH, GLA_KEY), F32),
        ],
        compiler_params=_params(("parallel",), V7X_VMEM_LIMIT),
        name="gla",
    )(hb, hb, hb, hb, lr, wg_pad, bg, nw)


_BUCKET_STEPS = (12, 16, 23, 32, 46, 64, 91)


def _bias_kernel(rb_ref, out_ref):
    r = lax.broadcasted_iota(jnp.int32, (Q_SUB, KEY_BLOCK), 0)
    l = lax.broadcasted_iota(jnp.int32, (Q_SUB, KEY_BLOCK), 1)
    for s in range(BIAS_SLOTS):
        if s == 0:
            rel = jnp.full((Q_SUB, KEY_BLOCK), -SEQ, jnp.int32)
        elif s == BIAS_SLOTS - 1:
            rel = jnp.full((Q_SUB, KEY_BLOCK), SEQ, jnp.int32)
        else:
            rel = (s - 2) * KEY_BLOCK + l - r
        n = jnp.abs(rel)
        large = jnp.full_like(n, 8)
        for t in _BUCKET_STEPS:
            large = large + (n >= t).astype(jnp.int32)
        bucket = jnp.where(rel > 0, REL_BUCKETS // 2, 0) + jnp.where(n < 8, n, large)
        for h in range(DIFF_HEADS):
            acc = jnp.zeros((Q_SUB, KEY_BLOCK), F32)
            for bkt in range(REL_BUCKETS):
                acc = jnp.where(bucket == bkt, rb_ref[bkt, h], acc)
            out_ref[h, s] = acc * LOG2E


def _bias_tiles(rel_bias):
    return pl.pallas_call(
        _bias_kernel,
        in_specs=[pl.BlockSpec(memory_space=pltpu.SMEM)],
        out_specs=pl.BlockSpec(memory_space=pltpu.VMEM),
        out_shape=jax.ShapeDtypeStruct((DIFF_HEADS, BIAS_SLOTS, Q_SUB, KEY_BLOCK), F32),
        name="bias_tiles",
    )(rel_bias)


def _diff_kernel(lam_init, q_ref, k_ref, v_ref, bias_ref, lam_ref, nw_ref, out_ref, kt_ref, v1_ref):
    qi = pl.program_id(2)

    @pl.when(qi == 0)
    def _():
        kt_ref[...] = k_ref[...].T
        v1_ref[:, :DIFF_HEAD_V] = v_ref[...]
        v1_ref[:, DIFF_HEAD_V:] = jnp.ones((SEQ, DIFF_HEAD_V), BF16)

    lam = lam_ref[...]
    lam_full = (jnp.exp(jnp.sum(lam[0:1] * lam[1:2], axis=-1, keepdims=True))
                - jnp.exp(jnp.sum(lam[2:3] * lam[3:4], axis=-1, keepdims=True)) + lam_init)
    lane = lax.broadcasted_iota(jnp.int32, (Q_SUB, DIFF_HEAD_V), 1)
    zero = jnp.zeros((Q_SUB, DIFF_HEAD_V), BF16)
    kt = kt_ref[...]
    subs = Q_TILE // Q_SUB
    blocks_per_sub = Q_SUB // KEY_BLOCK
    s = []
    for u in range(subs):
        q = q_ref[u * Q_SUB:(u + 1) * Q_SUB, :]
        s.append([_dot(jnp.where((lane // DIFF_HEAD_DIM) == c, q, zero), kt) for c in range(2)])
    for u in range(subs):
        tile = subs * qi + u
        bias = jnp.concatenate(
            [bias_ref[jnp.clip(kb - blocks_per_sub * tile + 2, 0, BIAS_SLOTS - 1)]
             for kb in range(SEQ // KEY_BLOCK)], axis=1)
        e = []
        for c in range(2):
            sc = s[u][c] + bias
            e.append(jnp.exp2(sc - jnp.max(sc, axis=-1, keepdims=True)).astype(BF16))
        ol = [_dot(e[c], v1_ref[...]) for c in range(2)]
        r = [ol[c][:, :DIFF_HEAD_V] / ol[c][:, DIFF_HEAD_V:] for c in range(2)]
        o = r[0] - lam_full * r[1]
        ms = jnp.mean(o * o, axis=-1, keepdims=True)
        out_ref[u * Q_SUB:(u + 1) * Q_SUB, :] = (
            o * lax.rsqrt(ms + RMS_EPS) * nw_ref[...] * (1.0 - lam_init)).astype(BF16)


def _diff(hb, bias_tiles, lam, nw, lam_init):
    nq = SEQ // Q_TILE
    qcol, kcol, vcol = 1536 // 128, 2048 // 128, 2560 // 128
    return pl.pallas_call(
        functools.partial(_diff_kernel, lam_init),
        grid=(BATCH, DIFF_HEADS, nq),
        in_specs=[
            pl.BlockSpec((Q_TILE, DIFF_HEAD_V), lambda b, h, i: (b * nq + i, qcol + h)),
            pl.BlockSpec((SEQ, DIFF_HEAD_V), lambda b, h, i: (b, kcol + h)),
            pl.BlockSpec((SEQ, DIFF_HEAD_V), lambda b, h, i: (b, vcol + h)),
            pl.BlockSpec((None, BIAS_SLOTS, Q_SUB, KEY_BLOCK), lambda b, h, i: (h, 0, 0, 0)),
            pl.BlockSpec((4, DIFF_HEAD_DIM), lambda b, h, i: (0, 0)),
            pl.BlockSpec((1, DIFF_HEAD_V), lambda b, h, i: (0, 0)),
        ],
        out_specs=pl.BlockSpec((Q_TILE, DIFF_HEAD_V), lambda b, h, i: (b * nq + i, h)),
        out_shape=jax.ShapeDtypeStruct((TOKENS, DIFF_WIDTH), BF16),
        scratch_shapes=[
            pltpu.VMEM((DIFF_HEAD_V, SEQ), BF16),
            pltpu.VMEM((SEQ, 2 * DIFF_HEAD_V), BF16),
        ],
        compiler_params=_params(("parallel", "parallel", "arbitrary"), V7X_VMEM_LIMIT),
        name="diff_attn",
    )(hb, hb, hb, bias_tiles, lam, nw)


OUTPROJ_TM = 512


def _outproj_kernel(og_ref, od_ref, w_ref, x_ref, g_ref, b_ref, out_ref, words_ref):
    mix = _dot(og_ref[...], w_ref[0:GLA_WIDTH, :]) + _dot(od_ref[...], w_ref[GLA_WIDTH:, :])
    x1 = _layer_norm(ALPHA * x_ref[...] + mix, g_ref[...], b_ref[...])
    out_ref[...] = x1
    words_ref[...] = _to_storage(_pack_words(x1), TOKEN_B)


def _outproj_ln(o_gla, o_diff, w_out, x2, g, b):
    tm = OUTPROJ_TM
    return pl.pallas_call(
        _outproj_kernel,
        grid=(TOKENS // tm,),
        in_specs=[
            pl.BlockSpec((tm, GLA_WIDTH), lambda i: (i, 0)),
            pl.BlockSpec((tm, DIFF_WIDTH), lambda i: (i, 0)),
            pl.BlockSpec((D_MODEL, D_MODEL), lambda i: (0, 0)),
            pl.BlockSpec((tm, D_MODEL), lambda i: (i, 0)),
            pl.BlockSpec((1, D_MODEL), lambda i: (0, 0)),
            pl.BlockSpec((1, D_MODEL), lambda i: (0, 0)),
        ],
        out_specs=[
            pl.BlockSpec((tm, D_MODEL), lambda i: (i, 0)),
            pl.BlockSpec((tm * PIECES, PIECE), lambda i: (i, 0)),
        ],
        out_shape=[
            jax.ShapeDtypeStruct((TOKENS, D_MODEL), F32),
            jax.ShapeDtypeStruct((TOKENS * PIECES, PIECE), U32),
        ],
        compiler_params=_params(("parallel",), V7X_VMEM_LIMIT),
        name="outproj_ln",
    )(o_gla, o_diff, w_out, x2, g, b)


def _pool_kernel(x_ref, w_ref, sc_ref, g_ref, b_ref, out_ref, words_ref):
    t = pl.program_id(1)
    r0 = pl.multiple_of(t * POOL_TILE, POOL_TILE)
    w0 = pl.multiple_of(jnp.clip(r0 - POOL_HALO, 0, SEQ - POOL_WIN), POOL_HALO)
    xt = x_ref[pl.ds(r0, POOL_TILE), :]
    xw_hi, xw_lo = _split_bf16(x_ref[pl.ds(w0, POOL_WIN), :])
    rel = ((w0 - r0) + lax.broadcasted_iota(jnp.int32, (POOL_TILE, POOL_WIN), 1)
           - lax.broadcasted_iota(jnp.int32, (POOL_TILE, POOL_WIN), 0))
    posc = r0 + lax.broadcasted_iota(jnp.int32, (POOL_TILE, 1), 0)
    ys = []
    for gi, win in enumerate(POOL_WINDOWS):
        half = win // 2
        band = jnp.where((rel >= -half) & (rel < half), 1.0, 0.0).astype(BF16)
        cols = slice(gi * POOL_GROUP_W, (gi + 1) * POOL_GROUP_W)
        wsum = _dot(band, xw_hi[:, cols]) + _dot(band, xw_lo[:, cols])
        count = (jnp.minimum(posc + half, SEQ) - jnp.maximum(posc - half, 0)).astype(F32)
        pooled = wsum / count - xt[:, cols]
        ys.append(_dot(pooled.astype(BF16), w_ref[gi]))
    mix = jnp.concatenate(ys, axis=1) * sc_ref[...]
    x1 = _layer_norm(ALPHA * xt + mix, g_ref[...], b_ref[...])
    out_ref[...] = x1
    words_ref[...] = _to_storage(_pack_words(x1), TOKEN_B)


def _pool_ln(x3, w_pool, scale, g, b):
    nt = SEQ // POOL_TILE
    return pl.pallas_call(
        _pool_kernel,
        grid=(BATCH, nt),
        in_specs=[
            pl.BlockSpec((None, SEQ, D_MODEL), lambda bi, t: (bi, 0, 0)),
            pl.BlockSpec((POOL_GROUPS, POOL_GROUP_W, POOL_GROUP_W), lambda bi, t: (0, 0, 0)),
            pl.BlockSpec((1, D_MODEL), lambda bi, t: (0, 0)),
            pl.BlockSpec((1, D_MODEL), lambda bi, t: (0, 0)),
            pl.BlockSpec((1, D_MODEL), lambda bi, t: (0, 0)),
        ],
        out_specs=[
            pl.BlockSpec((POOL_TILE, D_MODEL), lambda bi, t: (bi * nt + t, 0)),
            pl.BlockSpec((POOL_TILE * PIECES, PIECE), lambda bi, t: (bi * nt + t, 0)),
        ],
        out_shape=[
            jax.ShapeDtypeStruct((TOKENS, D_MODEL), F32),
            jax.ShapeDtypeStruct((TOKENS * PIECES, PIECE), U32),
        ],
        compiler_params=_params(("parallel", "arbitrary"), V7X_VMEM_LIMIT),
        name="pool_ln",
    )(x3, w_pool, scale, g, b)


ROUTER_TM = 512


def _router_kernel(x_ref, wt_ref, rb_ref, eid_ref, wcol_ref):
    x_hi, x_lo = _split_bf16(x_ref[...])
    w_hi, w_lo = _split_bf16(wt_ref[...])
    logits = _dot_nt(w_hi, x_hi) + _dot_nt(w_hi, x_lo) + _dot_nt(w_lo, x_hi)
    aff = 1.0 / (1.0 + jnp.exp(-logits))
    sel = aff + rb_ref[...]
    a = [aff[e:e + 1, :] for e in range(N_EXPERTS)]
    s = [sel[e:e + 1, :] for e in range(N_EXPERTS)]
    one = jnp.ones_like(s[0])
    zero = jnp.zeros_like(s[0])
    rank = [zero] * N_EXPERTS
    for gi in range(N_EXPERT_GROUPS):
        for i in range(EXPERTS_PER_GROUP):
            for j in range(i + 1, EXPERTS_PER_GROUP):
                ei, ej = gi * EXPERTS_PER_GROUP + i, gi * EXPERTS_PER_GROUP + j
                j_wins = s[ej] > s[ei]
                rank[ei] = rank[ei] + jnp.where(j_wins, one, zero)
                rank[ej] = rank[ej] + jnp.where(j_wins, zero, one)
    top2 = [rank[e] < 2.0 for e in range(N_EXPERTS)]
    score = []
    for gi in range(N_EXPERT_GROUPS):
        acc = zero
        for i in range(EXPERTS_PER_GROUP):
            e = gi * EXPERTS_PER_GROUP + i
            acc = acc + jnp.where(top2[e], s[e], zero)
        score.append(acc)
    best, best_g = score[0], zero
    for gi in range(1, N_EXPERT_GROUPS):
        better = score[gi] > best
        best = jnp.where(better, score[gi], best)
        best_g = jnp.where(better, float(gi) * one, best_g)
    picked = [top2[e] & (best_g == float(e // EXPERTS_PER_GROUP)) for e in range(N_EXPERTS)]
    e_lo, e_hi = 99.0 * one, -one
    for e in range(N_EXPERTS):
        e_lo = jnp.where(picked[e], jnp.minimum(e_lo, float(e)), e_lo)
        e_hi = jnp.where(picked[e], jnp.maximum(e_hi, float(e)), e_hi)
    a_lo, a_hi = zero, zero
    for e in range(N_EXPERTS):
        a_lo = jnp.where(picked[e] & (e_lo == float(e)), a[e], a_lo)
        a_hi = jnp.where(picked[e] & (e_hi == float(e)), a[e], a_hi)
    denom = a_lo + a_hi
    eid_ref[...] = jnp.concatenate([e_lo, e_hi], axis=0).astype(I32)
    w8 = jnp.concatenate([a_lo / denom, a_hi / denom] + [zero] * 6, axis=0)
    wcol_ref[...] = w8.T


def _router(x1, w_router_t, router_bias_col):
    tm = ROUTER_TM
    return pl.pallas_call(
        _router_kernel,
        grid=(TOKENS // tm,),
        in_specs=[
            pl.BlockSpec((tm, D_MODEL), lambda i: (i, 0)),
            pl.BlockSpec((N_EXPERTS, D_MODEL), lambda i: (0, 0)),
            pl.BlockSpec((N_EXPERTS, 1), lambda i: (0, 0)),
        ],
        out_specs=[
            pl.BlockSpec((TOP_K, tm), lambda i: (0, i)),
            pl.BlockSpec((tm, 8), lambda i: (i, 0)),
        ],
        out_shape=[
            jax.ShapeDtypeStruct((TOP_K, TOKENS), I32),
            jax.ShapeDtypeStruct((TOKENS, 8), F32),
        ],
        compiler_params=_params(("parallel",), V7X_VMEM_LIMIT),
        name="router",
    )(x1, w_router_t, router_bias_col)


PLAN_CHUNK = 512


def _plan_kernel(eid_ref, pos_ref, meta_ref):
    n_chunks = TOKENS // PLAN_CHUNK
    erow = lax.broadcasted_iota(I32, (N_EXPERTS, PLAN_CHUNK), 0)

    def lanes(c):
        return pl.ds(pl.multiple_of(c * PLAN_CHUNK, PLAN_CHUNK), PLAN_CHUNK)

    def onehot(k, c):
        return erow == eid_ref[k:k + 1, lanes(c)]

    counts = jnp.zeros((N_EXPERTS, 1), F32)
    for k in range(TOP_K):
        counts = lax.fori_loop(
            0, n_chunks,
            lambda c, cnt, k=k: cnt + jnp.sum(jnp.where(onehot(k, c), 1.0, 0.0), axis=1, keepdims=True),
            counts)
    padded = jnp.ceil(counts * (1.0 / EXPERT_TM)) * EXPERT_TM
    ei = lax.broadcasted_iota(I32, (N_EXPERTS, N_EXPERTS), 0)
    ej = lax.broadcasted_iota(I32, (N_EXPERTS, N_EXPERTS), 1)
    padded_row = jnp.sum(jnp.where(ei == ej, padded, 0.0), axis=0, keepdims=True)
    start = jnp.sum(jnp.where(ej < ei, padded_row, 0.0), axis=1, keepdims=True)
    end = start + padded
    total = jnp.sum(padded, axis=0, keepdims=True)

    tri = jnp.where(lax.broadcasted_iota(I32, (PLAN_CHUNK, PLAN_CHUNK), 0)
                    <= lax.broadcasted_iota(I32, (PLAN_CHUNK, PLAN_CHUNK), 1), 1.0, 0.0).astype(BF16)

    def pos_body(c, carry, k):
        oh = onehot(k, c)
        ohf = jnp.where(oh, 1.0, 0.0)
        prefix = _dot(ohf.astype(BF16), tri)
        pos = jnp.sum(jnp.where(oh, start + carry + prefix - 1.0, 0.0), axis=0, keepdims=True)
        pos_ref[k:k + 1, lanes(c)] = pos.astype(I32)
        return carry + jnp.sum(ohf, axis=1, keepdims=True)

    carry = jnp.zeros((N_EXPERTS, 1), F32)
    for k in range(TOP_K):
        carry = lax.fori_loop(0, n_chunks, functools.partial(pos_body, k=k), carry)

    tile0 = lax.broadcasted_iota(I32, (1, META_LANES), 1).astype(F32) * EXPERT_TM
    n_used = total * (1.0 / EXPERT_TM)
    last = n_used - 1.0
    tile_c = jnp.minimum(tile0, last * EXPERT_TM)
    expert = jnp.sum(jnp.where(end <= tile_c, 1.0, 0.0), axis=0, keepdims=True)
    block = tile_c * (1.0 / EXPERT_TM)
    meta = jnp.concatenate([expert, block, jnp.broadcast_to(n_used, (1, META_LANES))]
                           + [jnp.zeros((1, META_LANES), F32)] * 5, axis=0)
    meta_ref[...] = meta.astype(I32)


def _plan(eid):
    return pl.pallas_call(
        _plan_kernel,
        in_specs=[pl.BlockSpec(memory_space=pltpu.VMEM)],
        out_specs=[pl.BlockSpec(memory_space=pltpu.VMEM), pl.BlockSpec(memory_space=pltpu.VMEM)],
        out_shape=[
            jax.ShapeDtypeStruct((TOP_K, TOKENS), I32),
            jax.ShapeDtypeStruct((8, META_LANES), I32),
        ],
        name="plan",
    )(eid)


def _sc_mesh():
    return plsc.VectorSubcoreMesh(core_axis_name="c", subcore_axis_name="s")


def _sc_worker_id():
    return lax.axis_index("s") * SC_CORES + lax.axis_index("c")


TOKEN_CHUNKS = TOKENS * PIECES // SC_CHUNK
CHUNKS_PER_WORKER = TOKEN_CHUNKS // SC_WORKERS


def _dispatch(words, didx):
    @functools.partial(
        pl.kernel,
        out_type=jax.ShapeDtypeStruct((SORTED_ROWS * PIECES, PIECE), U32),
        mesh=_sc_mesh(),
        scratch_types=[
            pltpu.VMEM((TOP_K, CHUNKS_PER_WORKER, SC_CHUNK), I32),
            pltpu.VMEM((SC_CHUNK, PIECE), U32),
        ],
        name="moe_dispatch",
    )
    def run(words_hbm, didx_hbm, out_hbm, idx_v, buf):
        c0 = _sc_worker_id() * CHUNKS_PER_WORKER
        for k in range(TOP_K):
            pltpu.sync_copy(didx_hbm.at[k, pl.ds(c0, CHUNKS_PER_WORKER)], idx_v.at[k])

        @pl.loop(0, CHUNKS_PER_WORKER)
        def _(c):
            pltpu.sync_copy(words_hbm.at[pl.ds((c0 + c) * SC_CHUNK, SC_CHUNK)], buf)
            for k in range(TOP_K):
                pltpu.sync_copy(buf, out_hbm.at[idx_v.at[k, c]])

    return run(words, didx)


def _combine(sorted_out, didx):
    @functools.partial(
        pl.kernel,
        out_type=jax.ShapeDtypeStruct((TOP_K, TOKENS * PIECES, PIECE), U32),
        mesh=_sc_mesh(),
        scratch_types=[
            pltpu.VMEM((TOP_K, CHUNKS_PER_WORKER, SC_CHUNK), I32),
            pltpu.VMEM((SC_CHUNK, PIECE), U32),
        ],
        name="moe_combine",
    )
    def run(src_hbm, didx_hbm, out_hbm, idx_v, buf):
        c0 = _sc_worker_id() * CHUNKS_PER_WORKER
        for k in range(TOP_K):
            pltpu.sync_copy(didx_hbm.at[k, pl.ds(c0, CHUNKS_PER_WORKER)], idx_v.at[k])

        @pl.loop(0, CHUNKS_PER_WORKER)
        def _(c):
            for k in range(TOP_K):
                pltpu.sync_copy(src_hbm.at[idx_v.at[k, c]], buf)
                pltpu.sync_copy(buf, out_hbm.at[k, pl.ds((c0 + c) * SC_CHUNK, SC_CHUNK)])

    return run(sorted_out, didx)


def _expert_kernel(te_ref, tb_ref, nu_ref, xs_ref, wg_ref, wu_ref, wd_ref, out_ref, wgb, wub, wdb):
    i = pl.program_id(0)
    fresh = (i == 0) | (te_ref[i] != te_ref[jnp.maximum(i - 1, 0)])

    @pl.when(fresh)
    def _():
        wgb[...] = wg_ref[...].astype(BF16)
        wub[...] = wu_ref[...].astype(BF16)
        wdb[...] = wd_ref[...].astype(BF16)

    @pl.when(i < nu_ref[0])
    def _():
        lo, hi = _unpack_words(_from_storage(xs_ref[...], EXPERT_TM))
        lo, hi = lo.astype(BF16), hi.astype(BF16)
        hg = _dot(lo, wgb[0:WORD_COLS, :]) + _dot(hi, wgb[WORD_COLS:, :])
        hu = _dot(lo, wub[0:WORD_COLS, :]) + _dot(hi, wub[WORD_COLS:, :])
        h = (hg / (1.0 + jnp.exp(-hg)) * hu).astype(BF16)
        y = _dot(h, wdb[...])
        out_ref[...] = _to_storage(_pack_words(y), EXPERT_TM)


def _experts(te, tb, nu, xs, w_gate, w_up, w_down, layer):
    wspec = pl.BlockSpec((None, None, D_MODEL, D_MODEL), lambda i, te, tb, nu: (layer, te[i], 0, 0))
    rows = pl.BlockSpec((EXPERT_TM * PIECES, PIECE), lambda i, te, tb, nu: (tb[i], 0))
    return pl.pallas_call(
        _expert_kernel,
        grid_spec=pltpu.PrefetchScalarGridSpec(
            num_scalar_prefetch=3,
            grid=(N_TILES,),
            in_specs=[rows, wspec, wspec, wspec],
            out_specs=rows,
            scratch_shapes=[pltpu.VMEM((D_MODEL, D_MODEL), BF16)] * 3,
        ),
        out_shape=jax.ShapeDtypeStruct((SORTED_ROWS * PIECES, PIECE), U32),
        compiler_params=_params(("arbitrary",), V7X_VMEM_LIMIT),
        name="experts",
    )(te, tb, nu, xs, w_gate, w_up, w_down)


COMBINE_TM = 512


def _combine_ln_kernel(x_ref, y_ref, w_ref, g_ref, b_ref, out_ref):
    w = w_ref[...]
    ffn = None
    for k in range(TOP_K):
        lo, hi = _unpack_words(_from_storage(y_ref[k], TOKEN_B))
        yk = w[:, k:k + 1] * jnp.concatenate([lo, hi], axis=1)
        ffn = yk if ffn is None else ffn + yk
    out_ref[...] = _layer_norm(ALPHA * x_ref[...] + ffn, g_ref[...], b_ref[...])


def _combine_ln(x1, y_tok, wcol, g, b):
    tm = COMBINE_TM
    return pl.pallas_call(
        _combine_ln_kernel,
        grid=(TOKENS // tm,),
        in_specs=[
            pl.BlockSpec((tm, D_MODEL), lambda i: (i, 0)),
            pl.BlockSpec((TOP_K, tm * PIECES, PIECE), lambda i: (0, i, 0)),
            pl.BlockSpec((tm, 8), lambda i: (i, 0)),
            pl.BlockSpec((1, D_MODEL), lambda i: (0, 0)),
            pl.BlockSpec((1, D_MODEL), lambda i: (0, 0)),
        ],
        out_specs=pl.BlockSpec((tm, D_MODEL), lambda i: (i, 0)),
        out_shape=jax.ShapeDtypeStruct((TOKENS, D_MODEL), F32),
        compiler_params=_params(("parallel",), V7X_VMEM_LIMIT),
        name="combine_ln",
    )(x1, y_tok, wcol, g, b)


def _sorted_storage_rows(pos):
    p = pos.reshape(TOP_K, TOKENS // TOKEN_B, 1, TOKEN_B)
    j = jnp.arange(PIECES, dtype=I32).reshape(1, 1, PIECES, 1)
    r = (p // EXPERT_TM) * (PIECES * EXPERT_TM) + j * EXPERT_TM + p % EXPERT_TM
    return r.reshape(TOP_K, TOKEN_CHUNKS, SC_CHUNK)


def kernel(x, rel_bias, w_in, w_gla_gate, b_gla_gate, gla_norm, diff_lambda, diff_norm, w_out_mix, w_pool,
           pool_scale, ln_g, ln_b, w_router, router_bias, w_gate, w_up, w_down):
    x2 = x.reshape(TOKENS, D_MODEL)
    bias_tiles = _bias_tiles(rel_bias)
    w_router_t = w_router.T
    router_bias_col = router_bias.reshape(N_EXPERTS, 1)
    row = lambda v: v.reshape(1, -1)
    col_scale = jnp.concatenate([
        jnp.full((GLA_KEY,), GLA_HEAD_K ** -0.5, F32), jnp.ones((GLA_KEY + 2 * GLA_WIDTH,), F32),
        jnp.full((DIFF_WIDTH,), DIFF_HEAD_DIM ** -0.5 * LOG2E, F32), jnp.ones((2 * DIFF_WIDTH,), F32)])
    for layer in range(DEPTH):
        if layer % 2 == 0:
            i = layer // 2
            wi = w_in[i]
            w_main = (jnp.concatenate([wi[:, :1536], wi[:, 1568:]], axis=1) * col_scale).astype(BF16)
            w_lr = jnp.pad(wi[:, 1536:1568], ((0, 0), (0, LR_COLS - 2 * GATE_RANK))).astype(BF16)
            hb, lr = _inproj(x2, w_main, w_lr)
            wg_pad = jnp.zeros((2, LR_COLS, GLA_KEY), F32)
            wg_pad = wg_pad.at[0, 0:GATE_RANK].set(w_gla_gate[i, 0])
            wg_pad = wg_pad.at[1, GATE_RANK:2 * GATE_RANK].set(w_gla_gate[i, 1])
            o_gla = _gla(hb, lr, wg_pad, b_gla_gate[i], row(gla_norm[i]))
            lam_init = 0.8 - 0.6 * math.exp(-0.3 * layer)
            o_diff = _diff(hb, bias_tiles, diff_lambda[i], row(diff_norm[i]), lam_init)
            x1, words = _outproj_ln(o_gla, o_diff, w_out_mix[i].astype(BF16), x2,
                                    row(ln_g[layer, 0]), row(ln_b[layer, 0]))
        else:
            j = layer // 2
            x1, words = _pool_ln(x2.reshape(BATCH, SEQ, D_MODEL), w_pool[j].astype(BF16), row(pool_scale[j]),
                                 row(ln_g[layer, 0]), row(ln_b[layer, 0]))
        eid, wcol = _router(x1, w_router_t, router_bias_col)
        pos, meta = _plan(eid)
        didx = _sorted_storage_rows(pos)
        xs = _dispatch(words, didx)
        ys = _experts(meta[0, :N_TILES], meta[1, :N_TILES], meta[2, :1], xs, w_gate, w_up, w_down, layer)
        y_tok = _combine(ys, didx)
        x2 = _combine_ln(x1, y_tok, wcol, row(ln_g[layer, 1]), row(ln_b[layer, 1]))
    return x2.reshape(BATCH, SEQ, D_MODEL)
```

```python
import functools
import math

import jax
import jax.numpy as jnp
from jax import lax
from jax.experimental import pallas as pl
from jax.experimental.pallas import tpu as pltpu
from jax.experimental.pallas import tpu_sc as plsc

F32 = jnp.float32
BF16 = jnp.bfloat16
U32 = jnp.uint32
I32 = jnp.int32

D_MODEL = 1024
BATCH = 8
SEQ = 2048
DEPTH = 4
TOKENS = BATCH * SEQ

GLA_HEADS = 4
GLA_WIDTH = 512
GLA_HEAD_V = 128
GLA_KEY = 256
GLA_HEAD_K = 64
GATE_RANK = 16
GATE_NORMALIZER = 16.0
GLA_CHUNK = 64
GLA_BLOCK = 256
CHUNKS_PER_BLOCK = GLA_BLOCK // GLA_CHUNK

DIFF_HEADS = 4
DIFF_WIDTH = 512
DIFF_HEAD_V = 128
DIFF_HEAD_DIM = 64
REL_BUCKETS = 32
Q_TILE = 512
Q_SUB = 256
KEY_BLOCK = 128
BIAS_SLOTS = 6

POOL_WINDOWS = (2, 4, 8, 16)
POOL_GROUPS = 4
POOL_GROUP_W = 256
POOL_TILE = 256
POOL_HALO = 128
POOL_WIN = POOL_TILE + 2 * POOL_HALO

N_EXPERTS = 16
N_EXPERT_GROUPS = 4
EXPERTS_PER_GROUP = 4
TOP_K = 2

WORD_COLS = D_MODEL // 2
PIECE = 128
PIECES = WORD_COLS // PIECE
TOKEN_B = 128
EXPERT_TM = 512
PAIR_ROWS = TOKENS * TOP_K
SORTED_ROWS = PAIR_ROWS + N_EXPERTS * EXPERT_TM
N_TILES = SORTED_ROWS // EXPERT_TM
META_LANES = 128
assert N_TILES <= META_LANES

SC_CORES = 2
SC_SUBCORES = 16
SC_WORKERS = SC_CORES * SC_SUBCORES
SC_CHUNK = 128
SC_LOOKAHEAD = 2
DISPATCH_BUFS = 4
COMBINE_BUFS = 3

LOG2E = math.log2(math.e)
ALPHA = (2.0 * DEPTH) ** 0.25
LN_EPS = 1e-5
RMS_EPS = 1e-6

H_COLS = 3072
LR_COLS = 128

V7X_VMEM_LIMIT = 56 * 1024 * 1024


def _params(sem, vmem=None):
    return pltpu.CompilerParams(dimension_semantics=sem, vmem_limit_bytes=vmem)


def _split_bf16(x):
    hi = x.astype(BF16)
    lo = (x - hi.astype(F32)).astype(BF16)
    return hi, lo


def _dot(a, b):
    return jnp.dot(a, b, preferred_element_type=F32)


def _dot_nt(a, b):
    return lax.dot_general(a, b, (((1,), (1,)), ((), ())), preferred_element_type=F32)


def _dot_tn(a, b):
    return lax.dot_general(a, b, (((0,), (0,)), ((), ())), preferred_element_type=F32)


def _layer_norm(z, g, b):
    mu = jnp.mean(z, axis=-1, keepdims=True)
    zc = z - mu
    var = jnp.mean(zc * zc, axis=-1, keepdims=True)
    return zc * lax.rsqrt(var + LN_EPS) * g + b


def _pack_words(x):
    lo = lax.bitcast_convert_type(x[:, :WORD_COLS].astype(BF16).astype(F32), U32)
    hi = lax.bitcast_convert_type(x[:, WORD_COLS:].astype(BF16).astype(F32), U32)
    return (lo >> 16) | (hi & jnp.uint32(0xFFFF0000))


def _unpack_words(w):
    lo = lax.bitcast_convert_type(w << 16, F32)
    hi = lax.bitcast_convert_type(w & jnp.uint32(0xFFFF0000), F32)
    return lo, hi


def _to_storage(w, b):
    out = []
    for s in range(w.shape[0] // b):
        for j in range(PIECES):
            out.append(w[s * b:(s + 1) * b, j * PIECE:(j + 1) * PIECE])
    return jnp.concatenate(out, axis=0)


def _from_storage(r, b):
    rows = []
    for s in range(r.shape[0] // (b * PIECES)):
        base = s * b * PIECES
        rows.append(jnp.concatenate([r[base + j * b:base + (j + 1) * b, :] for j in range(PIECES)], axis=1))
    return jnp.concatenate(rows, axis=0) if len(rows) > 1 else rows[0]


INPROJ_TM = 512


def _inproj_kernel(x_ref, w_ref, wlr_ref, hb_ref, lr_ref):
    xb = x_ref[...].astype(BF16)
    hb_ref[...] = _dot(xb, w_ref[...]).astype(BF16)
    lr_ref[...] = _dot(xb, wlr_ref[...])


def _inproj(x2, w_main, w_lr):
    return pl.pallas_call(
        _inproj_kernel,
        grid=(TOKENS // INPROJ_TM,),
        in_specs=[
            pl.BlockSpec((INPROJ_TM, D_MODEL), lambda i: (i, 0)),
            pl.BlockSpec((D_MODEL, H_COLS), lambda i: (0, 0)),
            pl.BlockSpec((D_MODEL, LR_COLS), lambda i: (0, 0)),
        ],
        out_specs=[
            pl.BlockSpec((INPROJ_TM, H_COLS), lambda i: (i, 0)),
            pl.BlockSpec((INPROJ_TM, LR_COLS), lambda i: (i, 0)),
        ],
        out_shape=[
            jax.ShapeDtypeStruct((TOKENS, H_COLS), BF16),
            jax.ShapeDtypeStruct((TOKENS, LR_COLS), F32),
        ],
        compiler_params=_params(("parallel",), V7X_VMEM_LIMIT),
        name="inproj",
    )(x2, w_main, w_lr)


def _gla_kernel(q_ref, k_ref, v_ref, g_ref, lr_ref, wg_ref, bg_ref, nw_ref,
                out_ref, of_ref, ob_ref, sf_ref, sb_ref):
    blk, ch = GLA_BLOCK, GLA_CHUNK
    n_blocks = SEQ // blk
    row = lax.broadcasted_iota(jnp.int32, (blk, blk), 0)
    col = lax.broadcasted_iota(jnp.int32, (blk, blk), 1)
    same_chunk = (row // ch) == (col // ch)
    srow = lax.broadcasted_iota(jnp.int32, (GLA_WIDTH, GLA_KEY), 0) // GLA_HEAD_V
    scol = lax.broadcasted_iota(jnp.int32, (GLA_WIDTH, GLA_KEY), 1) // GLA_HEAD_K
    state_mask = srow == scol
    lane_head = lax.broadcasted_iota(jnp.int32, (blk, 2 * GLA_HEAD_K), 1) // GLA_HEAD_K

    def block_step(r0, direction, state_ref):
        fwd = direction == 0
        rows = pl.ds(r0, blk)
        lr_hi, lr_lo = _split_bf16(lr_ref[rows, :])
        w_hi, w_lo = _split_bf16(wg_ref[direction])
        logit = (_dot(lr_hi, w_hi) + _dot(lr_lo, w_hi) + _dot(lr_hi, w_lo)
                 + bg_ref[direction:direction + 1, :])
        la = (jnp.minimum(logit, 0.0) - jnp.log(1.0 + jnp.exp(-jnp.abs(logit)))) * (1.0 / GATE_NORMALIZER)
        la_hi, la_lo = _split_bf16(la)
        tri = jnp.where(same_chunk & ((col <= row) if fwd else (col >= row)), 1.0, 0.0).astype(BF16)
        b = _dot(tri, la_hi) + _dot(tri, la_lo)
        b3 = b.reshape(CHUNKS_PER_BLOCK, ch, GLA_KEY)
        mid = ch // 2 - 1 if fwd else ch // 2
        last = ch - 1 if fwd else 0
        b_mid = jnp.broadcast_to(b3[:, mid:mid + 1, :], b3.shape).reshape(blk, GLA_KEY)
        b_last = jnp.broadcast_to(b3[:, last:last + 1, :], b3.shape).reshape(blk, GLA_KEY)
        qf = q_ref[rows, :].astype(F32)
        kf = k_ref[rows, :].astype(F32)
        vb = v_ref[rows, :]
        qd = (qf * jnp.exp(b - b_mid)).astype(BF16)
        kd = (kf * jnp.exp(b_mid - b)).astype(BF16)
        qe = (qf * jnp.exp(b)).astype(BF16)
        kl = (kf * jnp.exp(b_last - b)).astype(BF16)
        keep = same_chunk & ((col <= row) if fwd else (col > row))
        intra = []
        for h in range(GLA_HEADS):
            pair = slice((h // 2) * 128, (h // 2) * 128 + 128)
            qh = jnp.where(lane_head == (h % 2), qd[:, pair], jnp.zeros_like(qd[:, pair]))
            sc = _dot_nt(qh, kd[:, pair])
            p = jnp.where(keep, sc, 0.0).astype(BF16)
            intra.append(_dot(p, vb[:, h * GLA_HEAD_V:(h + 1) * GLA_HEAD_V]))
        o_intra = jnp.concatenate(intra, axis=1)
        inter = [None] * CHUNKS_PER_BLOCK
        order = range(CHUNKS_PER_BLOCK) if fwd else range(CHUNKS_PER_BLOCK - 1, -1, -1)
        for c in order:
            cs = slice(c * ch, (c + 1) * ch)
            state = state_ref[...]
            inter[c] = _dot_nt(qe[cs, :], state.astype(BF16))
            d_state = _dot_tn(vb[cs, :], kl[cs, :])
            decay = jnp.exp(b3[c, last:last + 1, :])
            state_ref[...] = state * decay + jnp.where(state_mask, d_state, 0.0)
        return o_intra + jnp.concatenate(inter, axis=0)

    sf_ref[...] = jnp.zeros_like(sf_ref)
    sb_ref[...] = jnp.zeros_like(sb_ref)

    def scan_body(i, carry):
        rf = pl.multiple_of(i * blk, blk)
        rb = pl.multiple_of((n_blocks - 1 - i) * blk, blk)
        of_ref[pl.ds(rf, blk), :] = block_step(rf, 0, sf_ref)
        ob_ref[pl.ds(rb, blk), :] = block_step(rb, 1, sb_ref)
        return carry

    lax.fori_loop(0, n_blocks, scan_body, 0)

    def finish_body(i, carry):
        r0 = pl.multiple_of(i * blk, blk)
        o = of_ref[pl.ds(r0, blk), :] + ob_ref[pl.ds(r0, blk), :]
        gate = g_ref[pl.ds(r0, blk), :].astype(F32)
        gate = gate / (1.0 + jnp.exp(-gate))
        outs = []
        for h in range(GLA_HEADS):
            oh = o[:, h * GLA_HEAD_V:(h + 1) * GLA_HEAD_V]
            ms = jnp.mean(oh * oh, axis=-1, keepdims=True)
            outs.append(oh * lax.rsqrt(ms + RMS_EPS) * nw_ref[...])
        out_ref[pl.ds(r0, blk), :] = (jnp.concatenate(outs, axis=1) * gate).astype(BF16)
        return carry

    lax.fori_loop(0, n_blocks, finish_body, 0)


def _gla(hb, lr, wg_pad, bg, nw):
    return pl.pallas_call(
        _gla_kernel,
        grid=(BATCH,),
        in_specs=[
            pl.BlockSpec((SEQ, GLA_KEY), lambda b: (b, 0)),
            pl.BlockSpec((SEQ, GLA_KEY), lambda b: (b, 1)),
            pl.BlockSpec((SEQ, GLA_WIDTH), lambda b: (b, 1)),
            pl.BlockSpec((SEQ, GLA_WIDTH), lambda b: (b, 2)),
            pl.BlockSpec((SEQ, LR_COLS), lambda b: (b, 0)),
            pl.BlockSpec((2, LR_COLS, GLA_KEY), lambda b: (0, 0, 0)),
            pl.BlockSpec((2, GLA_KEY), lambda b: (0, 0)),
            pl.BlockSpec((1, GLA_HEAD_V), lambda b: (0, 0)),
        ],
        out_specs=pl.BlockSpec((SEQ, GLA_WIDTH), lambda b: (b, 0)),
        out_shape=jax.ShapeDtypeStruct((TOKENS, GLA_WIDTH), BF16),
        scratch_shapes=[
            pltpu.VMEM((SEQ, GLA_WIDTH), F32),
            pltpu.VMEM((SEQ, GLA_WIDTH), F32),
            pltpu.VMEM((GLA_WIDTH, GLA_KEY), F32),
            pltpu.VMEM((GLA_WIDTH, GLA_KEY), F32),
        ],
        compiler_params=_params(("parallel",), V7X_VMEM_LIMIT),
        name="gla",
    )(hb, hb, hb, hb, lr, wg_pad, bg, nw)


_BUCKET_STEPS = (12, 16, 23, 32, 46, 64, 91)


def _bias_kernel(rb_ref, out_ref):
    r = lax.broadcasted_iota(jnp.int32, (Q_SUB, KEY_BLOCK), 0)
    l = lax.broadcasted_iota(jnp.int32, (Q_SUB, KEY_BLOCK), 1)
    for s in range(BIAS_SLOTS):
        if s == 0:
            rel = jnp.full((Q_SUB, KEY_BLOCK), -SEQ, jnp.int32)
        elif s == BIAS_SLOTS - 1:
            rel = jnp.full((Q_SUB, KEY_BLOCK), SEQ, jnp.int32)
        else:
            rel = (s - 2) * KEY_BLOCK + l - r
        n = jnp.abs(rel)
        large = jnp.full_like(n, 8)
        for t in _BUCKET_STEPS:
            large = large + (n >= t).astype(jnp.int32)
        bucket = jnp.where(rel > 0, REL_BUCKETS // 2, 0) + jnp.where(n < 8, n, large)
        for h in range(DIFF_HEADS):
            acc = jnp.zeros((Q_SUB, KEY_BLOCK), F32)
            for bkt in range(REL_BUCKETS):
                acc = jnp.where(bucket == bkt, rb_ref[bkt, h], acc)
            out_ref[h, s] = acc * LOG2E


def _bias_tiles(rel_bias):
    return pl.pallas_call(
        _bias_kernel,
        in_specs=[pl.BlockSpec(memory_space=pltpu.SMEM)],
        out_specs=pl.BlockSpec(memory_space=pltpu.VMEM),
        out_shape=jax.ShapeDtypeStruct((DIFF_HEADS, BIAS_SLOTS, Q_SUB, KEY_BLOCK), F32),
        name="bias_tiles",
    )(rel_bias)


def _diff_kernel(lam_init, q_ref, k_ref, v_ref, bias_ref, lam_ref, nw_ref, out_ref, kt_ref, v1_ref):
    qi = pl.program_id(2)

    @pl.when(qi == 0)
    def _():
        kt_ref[...] = k_ref[...].T
        v1_ref[:, :DIFF_HEAD_V] = v_ref[...]
        v1_ref[:, DIFF_HEAD_V:] = jnp.ones((SEQ, DIFF_HEAD_V), BF16)

    lam = lam_ref[...]
    lam_full = (jnp.exp(jnp.sum(lam[0:1] * lam[1:2], axis=-1, keepdims=True))
                - jnp.exp(jnp.sum(lam[2:3] * lam[3:4], axis=-1, keepdims=True)) + lam_init)
    lane = lax.broadcasted_iota(jnp.int32, (Q_SUB, DIFF_HEAD_V), 1)
    zero = jnp.zeros((Q_SUB, DIFF_HEAD_V), BF16)
    kt = kt_ref[...]
    subs = Q_TILE // Q_SUB
    blocks_per_sub = Q_SUB // KEY_BLOCK
    s = []
    for u in range(subs):
        q = q_ref[u * Q_SUB:(u + 1) * Q_SUB, :]
        s.append([_dot(jnp.where((lane // DIFF_HEAD_DIM) == c, q, zero), kt) for c in range(2)])
    for u in range(subs):
        tile = subs * qi + u
        bias = jnp.concatenate(
            [bias_ref[jnp.clip(kb - blocks_per_sub * tile + 2, 0, BIAS_SLOTS - 1)]
             for kb in range(SEQ // KEY_BLOCK)], axis=1)
        e = []
        for c in range(2):
            sc = s[u][c] + bias
            e.append(jnp.exp2(sc - jnp.max(sc, axis=-1, keepdims=True)).astype(BF16))
        ol = [_dot(e[c], v1_ref[...]) for c in range(2)]
        r = [ol[c][:, :DIFF_HEAD_V] / ol[c][:, DIFF_HEAD_V:] for c in range(2)]
        o = r[0] - lam_full * r[1]
        ms = jnp.mean(o * o, axis=-1, keepdims=True)
        out_ref[u * Q_SUB:(u + 1) * Q_SUB, :] = (
            o * lax.rsqrt(ms + RMS_EPS) * nw_ref[...] * (1.0 - lam_init)).astype(BF16)


def _diff(hb, bias_tiles, lam, nw, lam_init):
    nq = SEQ // Q_TILE
    qcol, kcol, vcol = 1536 // 128, 2048 // 128, 2560 // 128
    return pl.pallas_call(
        functools.partial(_diff_kernel, lam_init),
        grid=(BATCH, DIFF_HEADS, nq),
        in_specs=[
            pl.BlockSpec((Q_TILE, DIFF_HEAD_V), lambda b, h, i: (b * nq + i, qcol + h)),
            pl.BlockSpec((SEQ, DIFF_HEAD_V), lambda b, h, i: (b, kcol + h)),
            pl.BlockSpec((SEQ, DIFF_HEAD_V), lambda b, h, i: (b, vcol + h)),
            pl.BlockSpec((None, BIAS_SLOTS, Q_SUB, KEY_BLOCK), lambda b, h, i: (h, 0, 0, 0)),
            pl.BlockSpec((4, DIFF_HEAD_DIM), lambda b, h, i: (0, 0)),
            pl.BlockSpec((1, DIFF_HEAD_V), lambda b, h, i: (0, 0)),
        ],
        out_specs=pl.BlockSpec((Q_TILE, DIFF_HEAD_V), lambda b, h, i: (b * nq + i, h)),
        out_shape=jax.ShapeDtypeStruct((TOKENS, DIFF_WIDTH), BF16),
        scratch_shapes=[
            pltpu.VMEM((DIFF_HEAD_V, SEQ), BF16),
            pltpu.VMEM((SEQ, 2 * DIFF_HEAD_V), BF16),
        ],
        compiler_params=_params(("parallel", "parallel", "arbitrary"), V7X_VMEM_LIMIT),
        name="diff_attn",
    )(hb, hb, hb, bias_tiles, lam, nw)


OUTPROJ_TM = 512


def _outproj_kernel(og_ref, od_ref, w_ref, x_ref, g_ref, b_ref, out_ref, words_ref):
    mix = _dot(og_ref[...], w_ref[0:GLA_WIDTH, :]) + _dot(od_ref[...], w_ref[GLA_WIDTH:, :])
    x1 = _layer_norm(ALPHA * x_ref[...] + mix, g_ref[...], b_ref[...])
    out_ref[...] = x1
    words_ref[...] = _to_storage(_pack_words(x1), TOKEN_B)


def _outproj_ln(o_gla, o_diff, w_out, x2, g, b):
    tm = OUTPROJ_TM
    return pl.pallas_call(
        _outproj_kernel,
        grid=(TOKENS // tm,),
        in_specs=[
            pl.BlockSpec((tm, GLA_WIDTH), lambda i: (i, 0)),
            pl.BlockSpec((tm, DIFF_WIDTH), lambda i: (i, 0)),
            pl.BlockSpec((D_MODEL, D_MODEL), lambda i: (0, 0)),
            pl.BlockSpec((tm, D_MODEL), lambda i: (i, 0)),
            pl.BlockSpec((1, D_MODEL), lambda i: (0, 0)),
            pl.BlockSpec((1, D_MODEL), lambda i: (0, 0)),
        ],
        out_specs=[
            pl.BlockSpec((tm, D_MODEL), lambda i: (i, 0)),
            pl.BlockSpec((tm * PIECES, PIECE), lambda i: (i, 0)),
        ],
        out_shape=[
            jax.ShapeDtypeStruct((TOKENS, D_MODEL), F32),
            jax.ShapeDtypeStruct((TOKENS * PIECES, PIECE), U32),
        ],
        compiler_params=_params(("parallel",), V7X_VMEM_LIMIT),
        name="outproj_ln",
    )(o_gla, o_diff, w_out, x2, g, b)


def _pool_kernel(x_ref, w_ref, sc_ref, g_ref, b_ref, out_ref, words_ref):
    t = pl.program_id(1)
    r0 = pl.multiple_of(t * POOL_TILE, POOL_TILE)
    w0 = pl.multiple_of(jnp.clip(r0 - POOL_HALO, 0, SEQ - POOL_WIN), POOL_HALO)
    xt = x_ref[pl.ds(r0, POOL_TILE), :]
    xw_hi, xw_lo = _split_bf16(x_ref[pl.ds(w0, POOL_WIN), :])
    rel = ((w0 - r0) + lax.broadcasted_iota(jnp.int32, (POOL_TILE, POOL_WIN), 1)
           - lax.broadcasted_iota(jnp.int32, (POOL_TILE, POOL_WIN), 0))
    posc = r0 + lax.broadcasted_iota(jnp.int32, (POOL_TILE, 1), 0)
    ys = []
    for gi, win in enumerate(POOL_WINDOWS):
        half = win // 2
        band = jnp.where((rel >= -half) & (rel < half), 1.0, 0.0).astype(BF16)
        cols = slice(gi * POOL_GROUP_W, (gi + 1) * POOL_GROUP_W)
        wsum = _dot(band, xw_hi[:, cols]) + _dot(band, xw_lo[:, cols])
        count = (jnp.minimum(posc + half, SEQ) - jnp.maximum(posc - half, 0)).astype(F32)
        pooled = wsum / count - xt[:, cols]
        ys.append(_dot(pooled.astype(BF16), w_ref[gi]))
    mix = jnp.concatenate(ys, axis=1) * sc_ref[...]
    x1 = _layer_norm(ALPHA * xt + mix, g_ref[...], b_ref[...])
    out_ref[...] = x1
    words_ref[...] = _to_storage(_pack_words(x1), TOKEN_B)


def _pool_ln(x3, w_pool, scale, g, b):
    nt = SEQ // POOL_TILE
    return pl.pallas_call(
        _pool_kernel,
        grid=(BATCH, nt),
        in_specs=[
            pl.BlockSpec((None, SEQ, D_MODEL), lambda bi, t: (bi, 0, 0)),
            pl.BlockSpec((POOL_GROUPS, POOL_GROUP_W, POOL_GROUP_W), lambda bi, t: (0, 0, 0)),
            pl.BlockSpec((1, D_MODEL), lambda bi, t: (0, 0)),
            pl.BlockSpec((1, D_MODEL), lambda bi, t: (0, 0)),
            pl.BlockSpec((1, D_MODEL), lambda bi, t: (0, 0)),
        ],
        out_specs=[
            pl.BlockSpec((POOL_TILE, D_MODEL), lambda bi, t: (bi * nt + t, 0)),
            pl.BlockSpec((POOL_TILE * PIECES, PIECE), lambda bi, t: (bi * nt + t, 0)),
        ],
        out_shape=[
            jax.ShapeDtypeStruct((TOKENS, D_MODEL), F32),
            jax.ShapeDtypeStruct((TOKENS * PIECES, PIECE), U32),
        ],
        compiler_params=_params(("parallel", "arbitrary"), V7X_VMEM_LIMIT),
        name="pool_ln",
    )(x3, w_pool, scale, g, b)


ROUTER_TM = 512


def _router_kernel(x_ref, wt_ref, rb_ref, eid_ref, wcol_ref):
    x_hi, x_lo = _split_bf16(x_ref[...])
    w_hi, w_lo = _split_bf16(wt_ref[...])
    logits = _dot_nt(w_hi, x_hi) + _dot_nt(w_hi, x_lo) + _dot_nt(w_lo, x_hi)
    aff = 1.0 / (1.0 + jnp.exp(-logits))
    sel = aff + rb_ref[...]
    a = [aff[e:e + 1, :] for e in range(N_EXPERTS)]
    s = [sel[e:e + 1, :] for e in range(N_EXPERTS)]
    one = jnp.ones_like(s[0])
    zero = jnp.zeros_like(s[0])
    rank = [zero] * N_EXPERTS
    for gi in range(N_EXPERT_GROUPS):
        for i in range(EXPERTS_PER_GROUP):
            for j in range(i + 1, EXPERTS_PER_GROUP):
                ei, ej = gi * EXPERTS_PER_GROUP + i, gi * EXPERTS_PER_GROUP + j
                j_wins = s[ej] > s[ei]
                rank[ei] = rank[ei] + jnp.where(j_wins, one, zero)
                rank[ej] = rank[ej] + jnp.where(j_wins, zero, one)
    top2 = [rank[e] < 2.0 for e in range(N_EXPERTS)]
    score = []
    for gi in range(N_EXPERT_GROUPS):
        acc = zero
        for i in range(EXPERTS_PER_GROUP):
            e = gi * EXPERTS_PER_GROUP + i
            acc = acc + jnp.where(top2[e], s[e], zero)
        score.append(acc)
    best, best_g = score[0], zero
    for gi in range(1, N_EXPERT_GROUPS):
        better = score[gi] > best
        best = jnp.where(better, score[gi], best)
        best_g = jnp.where(better, float(gi) * one, best_g)
    picked = [top2[e] & (best_g == float(e // EXPERTS_PER_GROUP)) for e in range(N_EXPERTS)]
    e_lo, e_hi = 99.0 * one, -one
    for e in range(N_EXPERTS):
        e_lo = jnp.where(picked[e], jnp.minimum(e_lo, float(e)), e_lo)
        e_hi = jnp.where(picked[e], jnp.maximum(e_hi, float(e)), e_hi)
    a_lo, a_hi = zero, zero
    for e in range(N_EXPERTS):
        a_lo = jnp.where(picked[e] & (e_lo == float(e)), a[e], a_lo)
        a_hi = jnp.where(picked[e] & (e_hi == float(e)), a[e], a_hi)
    denom = a_lo + a_hi
    eid_ref[...] = jnp.concatenate([e_lo, e_hi], axis=0).astype(I32)
    w8 = jnp.concatenate([a_lo / denom, a_hi / denom] + [zero] * 6, axis=0)
    wcol_ref[...] = w8.T


def _router(x1, w_router_t, router_bias_col):
    tm = ROUTER_TM
    return pl.pallas_call(
        _router_kernel,
        grid=(TOKENS // tm,),
        in_specs=[
            pl.BlockSpec((tm, D_MODEL), lambda i: (i, 0)),
            pl.BlockSpec((N_EXPERTS, D_MODEL), lambda i: (0, 0)),
            pl.BlockSpec((N_EXPERTS, 1), lambda i: (0, 0)),
        ],
        out_specs=[
            pl.BlockSpec((TOP_K, tm), lambda i: (0, i)),
            pl.BlockSpec((tm, 8), lambda i: (i, 0)),
        ],
        out_shape=[
            jax.ShapeDtypeStruct((TOP_K, TOKENS), I32),
            jax.ShapeDtypeStruct((TOKENS, 8), F32),
        ],
        compiler_params=_params(("parallel",), V7X_VMEM_LIMIT),
        name="router",
    )(x1, w_router_t, router_bias_col)


PLAN_CHUNK = 512


def _plan_kernel(eid_ref, pos_ref, meta_ref):
    n_chunks = TOKENS // PLAN_CHUNK
    erow = lax.broadcasted_iota(I32, (N_EXPERTS, PLAN_CHUNK), 0)

    def lanes(c):
        return pl.ds(pl.multiple_of(c * PLAN_CHUNK, PLAN_CHUNK), PLAN_CHUNK)

    def onehot(k, c):
        return erow == eid_ref[k:k + 1, lanes(c)]

    counts = jnp.zeros((N_EXPERTS, 1), F32)
    for k in range(TOP_K):
        counts = lax.fori_loop(
            0, n_chunks,
            lambda c, cnt, k=k: cnt + jnp.sum(jnp.where(onehot(k, c), 1.0, 0.0), axis=1, keepdims=True),
            counts)
    padded = jnp.ceil(counts * (1.0 / EXPERT_TM)) * EXPERT_TM
    ei = lax.broadcasted_iota(I32, (N_EXPERTS, N_EXPERTS), 0)
    ej = lax.broadcasted_iota(I32, (N_EXPERTS, N_EXPERTS), 1)
    padded_row = jnp.sum(jnp.where(ei == ej, padded, 0.0), axis=0, keepdims=True)
    start = jnp.sum(jnp.where(ej < ei, padded_row, 0.0), axis=1, keepdims=True)
    end = start + padded
    total = jnp.sum(padded, axis=0, keepdims=True)

    tri = jnp.where(lax.broadcasted_iota(I32, (PLAN_CHUNK, PLAN_CHUNK), 0)
                    <= lax.broadcasted_iota(I32, (PLAN_CHUNK, PLAN_CHUNK), 1), 1.0, 0.0).astype(BF16)

    def pos_body(c, carry, k):
        oh = onehot(k, c)
        ohf = jnp.where(oh, 1.0, 0.0)
        prefix = _dot(ohf.astype(BF16), tri)
        pos = jnp.sum(jnp.where(oh, start + carry + prefix - 1.0, 0.0), axis=0, keepdims=True)
        pos_ref[k:k + 1, lanes(c)] = pos.astype(I32)
        return carry + jnp.sum(ohf, axis=1, keepdims=True)

    carry = jnp.zeros((N_EXPERTS, 1), F32)
    for k in range(TOP_K):
        carry = lax.fori_loop(0, n_chunks, functools.partial(pos_body, k=k), carry)

    tile0 = lax.broadcasted_iota(I32, (1, META_LANES), 1).astype(F32) * EXPERT_TM
    n_used = total * (1.0 / EXPERT_TM)
    last = n_used - 1.0
    tile_c = jnp.minimum(tile0, last * EXPERT_TM)
    expert = jnp.sum(jnp.where(end <= tile_c, 1.0, 0.0), axis=0, keepdims=True)
    block = tile_c * (1.0 / EXPERT_TM)
    meta = jnp.concatenate([expert, block, jnp.broadcast_to(n_used, (1, META_LANES))]
                           + [jnp.zeros((1, META_LANES), F32)] * 5, axis=0)
    meta_ref[...] = meta.astype(I32)


def _plan(eid):
    return pl.pallas_call(
        _plan_kernel,
        in_specs=[pl.BlockSpec(memory_space=pltpu.VMEM)],
        out_specs=[pl.BlockSpec(memory_space=pltpu.VMEM), pl.BlockSpec(memory_space=pltpu.VMEM)],
        out_shape=[
            jax.ShapeDtypeStruct((TOP_K, TOKENS), I32),
            jax.ShapeDtypeStruct((8, META_LANES), I32),
        ],
        name="plan",
    )(eid)


def _sc_mesh():
    return plsc.VectorSubcoreMesh(core_axis_name="c", subcore_axis_name="s")


def _sc_worker_id():
    return lax.axis_index("s") * SC_CORES + lax.axis_index("c")


TOKEN_CHUNKS = TOKENS * PIECES // SC_CHUNK
CHUNKS_PER_WORKER = TOKEN_CHUNKS // SC_WORKERS


def _dispatch(words, didx):
    @functools.partial(
        pl.kernel,
        out_type=jax.ShapeDtypeStruct((SORTED_ROWS * PIECES, PIECE), U32),
        mesh=_sc_mesh(),
        scratch_types=[
            pltpu.VMEM((TOP_K, CHUNKS_PER_WORKER, SC_CHUNK), I32),
            pltpu.VMEM((DISPATCH_BUFS, SC_CHUNK, PIECE), U32),
            pltpu.SemaphoreType.DMA((DISPATCH_BUFS,)),
            pltpu.SemaphoreType.DMA((DISPATCH_BUFS, TOP_K)),
        ],
        name="moe_dispatch",
    )
    def run(words_hbm, didx_hbm, out_hbm, idx_v, bufs, rsem, wsem):
        c0 = _sc_worker_id() * CHUNKS_PER_WORKER
        for k in range(TOP_K):
            pltpu.sync_copy(didx_hbm.at[k, pl.ds(c0, CHUNKS_PER_WORKER)], idx_v.at[k])

        def read(c):
            b = c % DISPATCH_BUFS
            return pltpu.async_copy(words_hbm.at[pl.ds((c0 + c) * SC_CHUNK, SC_CHUNK)], bufs.at[b], rsem.at[b])

        def scatter(c):
            b = c % DISPATCH_BUFS
            return [pltpu.async_copy(bufs.at[b], out_hbm.at[idx_v.at[k, c]], wsem.at[b, k]) for k in range(TOP_K)]

        reads = {c: read(c) for c in range(SC_LOOKAHEAD)}
        scatters = {}
        for c in range(CHUNKS_PER_WORKER):
            if c >= DISPATCH_BUFS - SC_LOOKAHEAD:
                for d in scatters.pop(c - (DISPATCH_BUFS - SC_LOOKAHEAD)):
                    d.wait()
            if c + SC_LOOKAHEAD < CHUNKS_PER_WORKER:
                reads[c + SC_LOOKAHEAD] = read(c + SC_LOOKAHEAD)
            reads.pop(c).wait()
            scatters[c] = scatter(c)
        for ds in scatters.values():
            for d in ds:
                d.wait()

    return run(words, didx)


def _combine(sorted_out, didx):
    @functools.partial(
        pl.kernel,
        out_type=jax.ShapeDtypeStruct((TOP_K, TOKENS * PIECES, PIECE), U32),
        mesh=_sc_mesh(),
        scratch_types=[
            pltpu.VMEM((TOP_K, CHUNKS_PER_WORKER, SC_CHUNK), I32),
            pltpu.VMEM((COMBINE_BUFS, TOP_K, SC_CHUNK, PIECE), U32),
            pltpu.SemaphoreType.DMA((COMBINE_BUFS, TOP_K)),
            pltpu.SemaphoreType.DMA((COMBINE_BUFS, TOP_K)),
        ],
        name="moe_combine",
    )
    def run(src_hbm, didx_hbm, out_hbm, idx_v, bufs, rsem, wsem):
        c0 = _sc_worker_id() * CHUNKS_PER_WORKER
        for k in range(TOP_K):
            pltpu.sync_copy(didx_hbm.at[k, pl.ds(c0, CHUNKS_PER_WORKER)], idx_v.at[k])

        def gather(c):
            b = c % COMBINE_BUFS
            return [pltpu.async_copy(src_hbm.at[idx_v.at[k, c]], bufs.at[b, k], rsem.at[b, k]) for k in range(TOP_K)]

        def write(c):
            b = c % COMBINE_BUFS
            return [pltpu.async_copy(bufs.at[b, k], out_hbm.at[k, pl.ds((c0 + c) * SC_CHUNK, SC_CHUNK)],
                                     wsem.at[b, k]) for k in range(TOP_K)]

        gathers = {c: gather(c) for c in range(SC_LOOKAHEAD)}
        writes = {}
        for c in range(CHUNKS_PER_WORKER):
            if c >= COMBINE_BUFS - SC_LOOKAHEAD:
                for d in writes.pop(c - (COMBINE_BUFS - SC_LOOKAHEAD)):
                    d.wait()
            if c + SC_LOOKAHEAD < CHUNKS_PER_WORKER:
                gathers[c + SC_LOOKAHEAD] = gather(c + SC_LOOKAHEAD)
            for d in gathers.pop(c):
                d.wait()
            writes[c] = write(c)
        for ds in writes.values():
            for d in ds:
                d.wait()

    return run(sorted_out, didx)


def _expert_kernel(te_ref, tb_ref, nu_ref, xs_ref, wg_ref, wu_ref, wd_ref, out_ref, wgub, wdb):
    i = pl.program_id(0)
    fresh = (i == 0) | (te_ref[i] != te_ref[jnp.maximum(i - 1, 0)])

    @pl.when(fresh)
    def _():
        wgub[:, :D_MODEL] = wg_ref[...].astype(BF16)
        wgub[:, D_MODEL:] = wu_ref[...].astype(BF16)
        wdb[...] = wd_ref[...].astype(BF16)

    @pl.when(i < nu_ref[0])
    def _():
        lo, hi = _unpack_words(_from_storage(xs_ref[...], EXPERT_TM))
        x = jnp.concatenate([lo.astype(BF16), hi.astype(BF16)], axis=1)
        hgu = _dot(x, wgub[...])
        hg, hu = hgu[:, :D_MODEL], hgu[:, D_MODEL:]
        h = (hg / (1.0 + jnp.exp(-hg)) * hu).astype(BF16)
        y = _dot(h, wdb[...])
        out_ref[...] = _to_storage(_pack_words(y), EXPERT_TM)


def _experts(te, tb, nu, xs, w_gate, w_up, w_down, layer):
    wspec = pl.BlockSpec((None, None, D_MODEL, D_MODEL), lambda i, te, tb, nu: (layer, te[i], 0, 0))
    rows = pl.BlockSpec((EXPERT_TM * PIECES, PIECE), lambda i, te, tb, nu: (tb[i], 0))
    return pl.pallas_call(
        _expert_kernel,
        grid_spec=pltpu.PrefetchScalarGridSpec(
            num_scalar_prefetch=3,
            grid=(N_TILES,),
            in_specs=[rows, wspec, wspec, wspec],
            out_specs=rows,
            scratch_shapes=[pltpu.VMEM((D_MODEL, 2 * D_MODEL), BF16), pltpu.VMEM((D_MODEL, D_MODEL), BF16)],
        ),
        out_shape=jax.ShapeDtypeStruct((SORTED_ROWS * PIECES, PIECE), U32),
        compiler_params=_params(("arbitrary",), V7X_VMEM_LIMIT),
        name="experts",
    )(te, tb, nu, xs, w_gate, w_up, w_down)


COMBINE_TM = 512


def _combine_ln_kernel(x_ref, y_ref, w_ref, g_ref, b_ref, out_ref):
    w = w_ref[...]
    ffn = None
    for k in range(TOP_K):
        lo, hi = _unpack_words(_from_storage(y_ref[k], TOKEN_B))
        yk = w[:, k:k + 1] * jnp.concatenate([lo, hi], axis=1)
        ffn = yk if ffn is None else ffn + yk
    out_ref[...] = _layer_norm(ALPHA * x_ref[...] + ffn, g_ref[...], b_ref[...])


def _combine_ln(x1, y_tok, wcol, g, b):
    tm = COMBINE_TM
    return pl.pallas_call(
        _combine_ln_kernel,
        grid=(TOKENS // tm,),
        in_specs=[
            pl.BlockSpec((tm, D_MODEL), lambda i: (i, 0)),
            pl.BlockSpec((TOP_K, tm * PIECES, PIECE), lambda i: (0, i, 0)),
            pl.BlockSpec((tm, 8), lambda i: (i, 0)),
            pl.BlockSpec((1, D_MODEL), lambda i: (0, 0)),
            pl.BlockSpec((1, D_MODEL), lambda i: (0, 0)),
        ],
        out_specs=pl.BlockSpec((tm, D_MODEL), lambda i: (i, 0)),
        out_shape=jax.ShapeDtypeStruct((TOKENS, D_MODEL), F32),
        compiler_params=_params(("parallel",), V7X_VMEM_LIMIT),
        name="combine_ln",
    )(x1, y_tok, wcol, g, b)


def _sorted_storage_rows(pos):
    p = pos.reshape(TOP_K, TOKENS // TOKEN_B, 1, TOKEN_B)
    j = jnp.arange(PIECES, dtype=I32).reshape(1, 1, PIECES, 1)
    r = (p // EXPERT_TM) * (PIECES * EXPERT_TM) + j * EXPERT_TM + p % EXPERT_TM
    return r.reshape(TOP_K, TOKEN_CHUNKS, SC_CHUNK)


def kernel(x, rel_bias, w_in, w_gla_gate, b_gla_gate, gla_norm, diff_lambda, diff_norm, w_out_mix, w_pool,
           pool_scale, ln_g, ln_b, w_router, router_bias, w_gate, w_up, w_down):
    x2 = x.reshape(TOKENS, D_MODEL)
    bias_tiles = _bias_tiles(rel_bias)
    w_router_t = w_router.T
    router_bias_col = router_bias.reshape(N_EXPERTS, 1)
    row = lambda v: v.reshape(1, -1)
    col_scale = jnp.concatenate([
        jnp.full((GLA_KEY,), GLA_HEAD_K ** -0.5, F32), jnp.ones((GLA_KEY + 2 * GLA_WIDTH,), F32),
        jnp.full((DIFF_WIDTH,), DIFF_HEAD_DIM ** -0.5 * LOG2E, F32), jnp.ones((2 * DIFF_WIDTH,), F32)])
    for layer in range(DEPTH):
        if layer % 2 == 0:
            i = layer // 2
            wi = w_in[i]
            w_main = (jnp.concatenate([wi[:, :1536], wi[:, 1568:]], axis=1) * col_scale).astype(BF16)
            w_lr = jnp.pad(wi[:, 1536:1568], ((0, 0), (0, LR_COLS - 2 * GATE_RANK))).astype(BF16)
            hb, lr = _inproj(x2, w_main, w_lr)
            wg_pad = jnp.zeros((2, LR_COLS, GLA_KEY), F32)
            wg_pad = wg_pad.at[0, 0:GATE_RANK].set(w_gla_gate[i, 0])
            wg_pad = wg_pad.at[1, GATE_RANK:2 * GATE_RANK].set(w_gla_gate[i, 1])
            o_gla = _gla(hb, lr, wg_pad, b_gla_gate[i], row(gla_norm[i]))
            lam_init = 0.8 - 0.6 * math.exp(-0.3 * layer)
            o_diff = _diff(hb, bias_tiles, diff_lambda[i], row(diff_norm[i]), lam_init)
            x1, words = _outproj_ln(o_gla, o_diff, w_out_mix[i].astype(BF16), x2,
                                    row(ln_g[layer, 0]), row(ln_b[layer, 0]))
        else:
            j = layer // 2
            x1, words = _pool_ln(x2.reshape(BATCH, SEQ, D_MODEL), w_pool[j].astype(BF16), row(pool_scale[j]),
                                 row(ln_g[layer, 0]), row(ln_b[layer, 0]))
        eid, wcol = _router(x1, w_router_t, router_bias_col)
        pos, meta = _plan(eid)
        didx = _sorted_storage_rows(pos)
        xs = _dispatch(words, didx)
        ys = _experts(meta[0, :N_TILES], meta[1, :N_TILES], meta[2, :1], xs, w_gate, w_up, w_down, layer)
        y_tok = _combine(ys, didx)
        x2 = _combine_ln(x1, y_tok, wcol, row(ln_g[layer, 1]), row(ln_b[layer, 1]))
    return x2.reshape(BATCH, SEQ, D_MODEL)
```

```python
import functools
import math

import jax
import jax.numpy as jnp
from jax import lax
from jax.experimental import pallas as pl
from jax.experimental.pallas import tpu as pltpu
from jax.experimental.pallas import tpu_sc as plsc

F32 = jnp.float32
BF16 = jnp.bfloat16
U32 = jnp.uint32
I32 = jnp.int32

D_MODEL = 1024
BATCH = 8
SEQ = 2048
DEPTH = 4
TOKENS = BATCH * SEQ

GLA_HEADS = 4
GLA_WIDTH = 512
GLA_HEAD_V = 128
GLA_KEY = 256
GLA_HEAD_K = 64
GATE_RANK = 16
GATE_NORMALIZER = 16.0
GLA_CHUNK = 64
GLA_BLOCK = 256
CHUNKS_PER_BLOCK = GLA_BLOCK // GLA_CHUNK

DIFF_HEADS = 4
DIFF_WIDTH = 512
DIFF_HEAD_V = 128
DIFF_HEAD_DIM = 64
REL_BUCKETS = 32
Q_TILE = 512
Q_SUB = 256
KEY_BLOCK = 128
BIAS_SLOTS = 6

POOL_WINDOWS = (2, 4, 8, 16)
POOL_GROUPS = 4
POOL_GROUP_W = 256
POOL_TILE = 256
POOL_HALO = 64
POOL_WIN = POOL_TILE + 2 * POOL_HALO

N_EXPERTS = 16
N_EXPERT_GROUPS = 4
EXPERTS_PER_GROUP = 4
TOP_K = 2

WORD_COLS = D_MODEL // 2
PIECE = 128
PIECES = WORD_COLS // PIECE
TOKEN_B = 128
EXPERT_TM = 512
PAIR_ROWS = TOKENS * TOP_K
SORTED_ROWS = PAIR_ROWS + N_EXPERTS * EXPERT_TM
N_TILES = SORTED_ROWS // EXPERT_TM
META_LANES = 128
assert N_TILES <= META_LANES

SC_CORES = 2
SC_SUBCORES = 16
SC_WORKERS = SC_CORES * SC_SUBCORES
SC_CHUNK = 128
SC_LOOKAHEAD = 2
DISPATCH_BUFS = 4
COMBINE_BUFS = 3

LOG2E = math.log2(math.e)
ALPHA = (2.0 * DEPTH) ** 0.25
LN_EPS = 1e-5
RMS_EPS = 1e-6

H_COLS = 3072
LR_COLS = 128

V7X_VMEM_LIMIT = 56 * 1024 * 1024


def _params(sem, vmem=None):
    return pltpu.CompilerParams(dimension_semantics=sem, vmem_limit_bytes=vmem)


def _split_bf16(x):
    hi = x.astype(BF16)
    lo = (x - hi.astype(F32)).astype(BF16)
    return hi, lo


def _dot(a, b):
    return jnp.dot(a, b, preferred_element_type=F32)


def _dot_nt(a, b):
    return lax.dot_general(a, b, (((1,), (1,)), ((), ())), preferred_element_type=F32)


def _dot_tn(a, b):
    return lax.dot_general(a, b, (((0,), (0,)), ((), ())), preferred_element_type=F32)


def _layer_norm(z, g, b):
    mu = jnp.mean(z, axis=-1, keepdims=True)
    zc = z - mu
    var = jnp.mean(zc * zc, axis=-1, keepdims=True)
    return zc * lax.rsqrt(var + LN_EPS) * g + b


def _pack_words(x):
    lo = lax.bitcast_convert_type(x[:, :WORD_COLS].astype(BF16).astype(F32), U32)
    hi = lax.bitcast_convert_type(x[:, WORD_COLS:].astype(BF16).astype(F32), U32)
    return (lo >> 16) | (hi & jnp.uint32(0xFFFF0000))


def _unpack_words(w):
    lo = lax.bitcast_convert_type(w << 16, F32)
    hi = lax.bitcast_convert_type(w & jnp.uint32(0xFFFF0000), F32)
    return lo, hi


def _to_storage(w, b):
    out = []
    for s in range(w.shape[0] // b):
        for j in range(PIECES):
            out.append(w[s * b:(s + 1) * b, j * PIECE:(j + 1) * PIECE])
    return jnp.concatenate(out, axis=0)


def _from_storage(r, b):
    rows = []
    for s in range(r.shape[0] // (b * PIECES)):
        base = s * b * PIECES
        rows.append(jnp.concatenate([r[base + j * b:base + (j + 1) * b, :] for j in range(PIECES)], axis=1))
    return jnp.concatenate(rows, axis=0) if len(rows) > 1 else rows[0]


INPROJ_TM = 512


def _inproj_kernel(x_ref, w_ref, wlr_ref, hb_ref, lr_ref):
    xb = x_ref[...].astype(BF16)
    hb_ref[...] = _dot(xb, w_ref[...]).astype(BF16)
    lr_ref[...] = _dot(xb, wlr_ref[...])


def _inproj(x2, w_main, w_lr):
    return pl.pallas_call(
        _inproj_kernel,
        grid=(TOKENS // INPROJ_TM,),
        in_specs=[
            pl.BlockSpec((INPROJ_TM, D_MODEL), lambda i: (i, 0)),
            pl.BlockSpec((D_MODEL, H_COLS), lambda i: (0, 0)),
            pl.BlockSpec((D_MODEL, LR_COLS), lambda i: (0, 0)),
        ],
        out_specs=[
            pl.BlockSpec((INPROJ_TM, H_COLS), lambda i: (i, 0)),
            pl.BlockSpec((INPROJ_TM, LR_COLS), lambda i: (i, 0)),
        ],
        out_shape=[
            jax.ShapeDtypeStruct((TOKENS, H_COLS), BF16),
            jax.ShapeDtypeStruct((TOKENS, LR_COLS), F32),
        ],
        compiler_params=_params(("parallel",), V7X_VMEM_LIMIT),
        name="inproj",
    )(x2, w_main, w_lr)


def _gla_kernel(q_ref, k_ref, v_ref, g_ref, lr_ref, wg_ref, bg_ref, nw_ref,
                out_ref, of_ref, ob_ref, sf_ref, sb_ref):
    blk, ch = GLA_BLOCK, GLA_CHUNK
    n_blocks = SEQ // blk
    row = lax.broadcasted_iota(jnp.int32, (blk, blk), 0)
    col = lax.broadcasted_iota(jnp.int32, (blk, blk), 1)
    same_chunk = (row // ch) == (col // ch)
    srow = lax.broadcasted_iota(jnp.int32, (GLA_WIDTH, GLA_KEY), 0) // GLA_HEAD_V
    scol = lax.broadcasted_iota(jnp.int32, (GLA_WIDTH, GLA_KEY), 1) // GLA_HEAD_K
    state_mask = srow == scol
    lane_head = lax.broadcasted_iota(jnp.int32, (blk, 2 * GLA_HEAD_K), 1) // GLA_HEAD_K

    def block_step(r0, direction, state_ref):
        fwd = direction == 0
        rows = pl.ds(r0, blk)
        lr_hi, lr_lo = _split_bf16(lr_ref[rows, :])
        w_hi, w_lo = _split_bf16(wg_ref[direction])
        logit = (_dot(lr_hi, w_hi) + _dot(lr_lo, w_hi) + _dot(lr_hi, w_lo)
                 + bg_ref[direction:direction + 1, :])
        la = (jnp.minimum(logit, 0.0) - jnp.log(1.0 + jnp.exp(-jnp.abs(logit)))) * (1.0 / GATE_NORMALIZER)
        la_hi, la_lo = _split_bf16(la)
        tri = jnp.where(same_chunk & ((col <= row) if fwd else (col >= row)), 1.0, 0.0).astype(BF16)
        b = _dot(tri, la_hi) + _dot(tri, la_lo)
        b3 = b.reshape(CHUNKS_PER_BLOCK, ch, GLA_KEY)
        mid = ch // 2 - 1 if fwd else ch // 2
        last = ch - 1 if fwd else 0
        b_mid = jnp.broadcast_to(b3[:, mid:mid + 1, :], b3.shape).reshape(blk, GLA_KEY)
        b_last = jnp.broadcast_to(b3[:, last:last + 1, :], b3.shape).reshape(blk, GLA_KEY)
        qf = q_ref[rows, :].astype(F32)
        kf = k_ref[rows, :].astype(F32)
        vb = v_ref[rows, :]
        qd = (qf * jnp.exp(b - b_mid)).astype(BF16)
        kd = (kf * jnp.exp(b_mid - b)).astype(BF16)
        qe = (qf * jnp.exp(b)).astype(BF16)
        kl = (kf * jnp.exp(b_last - b)).astype(BF16)
        keep = same_chunk & ((col <= row) if fwd else (col > row))
        intra = []
        for h in range(GLA_HEADS):
            pair = slice((h // 2) * 128, (h // 2) * 128 + 128)
            qh = jnp.where(lane_head == (h % 2), qd[:, pair], jnp.zeros_like(qd[:, pair]))
            sc = _dot_nt(qh, kd[:, pair])
            p = jnp.where(keep, sc, 0.0).astype(BF16)
            intra.append(_dot(p, vb[:, h * GLA_HEAD_V:(h + 1) * GLA_HEAD_V]))
        o_intra = jnp.concatenate(intra, axis=1)
        inter = [None] * CHUNKS_PER_BLOCK
        order = range(CHUNKS_PER_BLOCK) if fwd else range(CHUNKS_PER_BLOCK - 1, -1, -1)
        for c in order:
            cs = slice(c * ch, (c + 1) * ch)
            state = state_ref[...]
            inter[c] = _dot_nt(qe[cs, :], state.astype(BF16))
            d_state = _dot_tn(vb[cs, :], kl[cs, :])
            decay = jnp.exp(b3[c, last:last + 1, :])
            state_ref[...] = state * decay + jnp.where(state_mask, d_state, 0.0)
        return o_intra + jnp.concatenate(inter, axis=0)

    sf_ref[...] = jnp.zeros_like(sf_ref)
    sb_ref[...] = jnp.zeros_like(sb_ref)

    def scan_body(i, carry):
        rf = pl.multiple_of(i * blk, blk)
        rb = pl.multiple_of((n_blocks - 1 - i) * blk, blk)
        of_ref[pl.ds(rf, blk), :] = block_step(rf, 0, sf_ref)
        ob_ref[pl.ds(rb, blk), :] = block_step(rb, 1, sb_ref)
        return carry

    lax.fori_loop(0, n_blocks, scan_body, 0)

    def finish_body(i, carry):
        r0 = pl.multiple_of(i * blk, blk)
        o = of_ref[pl.ds(r0, blk), :] + ob_ref[pl.ds(r0, blk), :]
        gate = g_ref[pl.ds(r0, blk), :].astype(F32)
        gate = gate / (1.0 + jnp.exp(-gate))
        outs = []
        for h in range(GLA_HEADS):
            oh = o[:, h * GLA_HEAD_V:(h + 1) * GLA_HEAD_V]
            ms = jnp.mean(oh * oh, axis=-1, keepdims=True)
            outs.append(oh * lax.rsqrt(ms + RMS_EPS) * nw_ref[...])
        out_ref[pl.ds(r0, blk), :] = (jnp.concatenate(outs, axis=1) * gate).astype(BF16)
        return carry

    lax.fori_loop(0, n_blocks, finish_body, 0)


def _gla(hb, lr, wg_pad, bg, nw):
    return pl.pallas_call(
        _gla_kernel,
        grid=(BATCH,),
        in_specs=[
            pl.BlockSpec((SEQ, GLA_KEY), lambda b: (b, 0)),
            pl.BlockSpec((SEQ, GLA_KEY), lambda b: (b, 1)),
            pl.BlockSpec((SEQ, GLA_WIDTH), lambda b: (b, 1)),
            pl.BlockSpec((SEQ, GLA_WIDTH), lambda b: (b, 2)),
            pl.BlockSpec((SEQ, LR_COLS), lambda b: (b, 0)),
            pl.BlockSpec((2, LR_COLS, GLA_KEY), lambda b: (0, 0, 0)),
            pl.BlockSpec((2, GLA_KEY), lambda b: (0, 0)),
            pl.BlockSpec((1, GLA_HEAD_V), lambda b: (0, 0)),
        ],
        out_specs=pl.BlockSpec((SEQ, GLA_WIDTH), lambda b: (b, 0)),
        out_shape=jax.ShapeDtypeStruct((TOKENS, GLA_WIDTH), BF16),
        scratch_shapes=[
            pltpu.VMEM((SEQ, GLA_WIDTH), F32),
            pltpu.VMEM((SEQ, GLA_WIDTH), F32),
            pltpu.VMEM((GLA_WIDTH, GLA_KEY), F32),
            pltpu.VMEM((GLA_WIDTH, GLA_KEY), F32),
        ],
        compiler_params=_params(("parallel",), V7X_VMEM_LIMIT),
        name="gla",
    )(hb, hb, hb, hb, lr, wg_pad, bg, nw)


_BUCKET_STEPS = (12, 16, 23, 32, 46, 64, 91)


def _bias_kernel(rb_ref, out_ref):
    r = lax.broadcasted_iota(jnp.int32, (Q_SUB, KEY_BLOCK), 0)
    l = lax.broadcasted_iota(jnp.int32, (Q_SUB, KEY_BLOCK), 1)
    for s in range(BIAS_SLOTS):
        if s == 0:
            rel = jnp.full((Q_SUB, KEY_BLOCK), -SEQ, jnp.int32)
        elif s == BIAS_SLOTS - 1:
            rel = jnp.full((Q_SUB, KEY_BLOCK), SEQ, jnp.int32)
        else:
            rel = (s - 2) * KEY_BLOCK + l - r
        n = jnp.abs(rel)
        large = jnp.full_like(n, 8)
        for t in _BUCKET_STEPS:
            large = large + (n >= t).astype(jnp.int32)
        bucket = jnp.where(rel > 0, REL_BUCKETS // 2, 0) + jnp.where(n < 8, n, large)
        for h in range(DIFF_HEADS):
            acc = jnp.zeros((Q_SUB, KEY_BLOCK), F32)
            for bkt in range(REL_BUCKETS):
                acc = jnp.where(bucket == bkt, rb_ref[bkt, h], acc)
            out_ref[h, s] = acc * LOG2E


def _bias_tiles(rel_bias):
    return pl.pallas_call(
        _bias_kernel,
        in_specs=[pl.BlockSpec(memory_space=pltpu.SMEM)],
        out_specs=pl.BlockSpec(memory_space=pltpu.VMEM),
        out_shape=jax.ShapeDtypeStruct((DIFF_HEADS, BIAS_SLOTS, Q_SUB, KEY_BLOCK), F32),
        name="bias_tiles",
    )(rel_bias)


def _diff_kernel(lam_init, q_ref, k_ref, v_ref, bias_ref, lam_ref, nw_ref, out_ref, kt_ref, v1_ref):
    qi = pl.program_id(2)

    @pl.when(qi == 0)
    def _():
        kt_ref[...] = k_ref[...].T
        v1_ref[:, :DIFF_HEAD_V] = v_ref[...]
        v1_ref[:, DIFF_HEAD_V:] = jnp.ones((SEQ, DIFF_HEAD_V), BF16)

    lam = lam_ref[...]
    lam_full = (jnp.exp(jnp.sum(lam[0:1] * lam[1:2], axis=-1, keepdims=True))
                - jnp.exp(jnp.sum(lam[2:3] * lam[3:4], axis=-1, keepdims=True)) + lam_init)
    lane = lax.broadcasted_iota(jnp.int32, (Q_SUB, DIFF_HEAD_V), 1)
    zero = jnp.zeros((Q_SUB, DIFF_HEAD_V), BF16)
    kt = kt_ref[...]
    subs = Q_TILE // Q_SUB
    blocks_per_sub = Q_SUB // KEY_BLOCK
    s = []
    for u in range(subs):
        q = q_ref[u * Q_SUB:(u + 1) * Q_SUB, :]
        s.append([_dot(jnp.where((lane // DIFF_HEAD_DIM) == c, q, zero), kt) for c in range(2)])
    for u in range(subs):
        tile = subs * qi + u
        bias = jnp.concatenate(
            [bias_ref[jnp.clip(kb - blocks_per_sub * tile + 2, 0, BIAS_SLOTS - 1)]
             for kb in range(SEQ // KEY_BLOCK)], axis=1)
        e = []
        for c in range(2):
            sc = s[u][c] + bias
            e.append(jnp.exp2(sc - jnp.max(sc, axis=-1, keepdims=True)).astype(BF16))
        ol = [_dot(e[c], v1_ref[...]) for c in range(2)]
        r = [ol[c][:, :DIFF_HEAD_V] / ol[c][:, DIFF_HEAD_V:] for c in range(2)]
        o = r[0] - lam_full * r[1]
        ms = jnp.mean(o * o, axis=-1, keepdims=True)
        out_ref[u * Q_SUB:(u + 1) * Q_SUB, :] = (
            o * lax.rsqrt(ms + RMS_EPS) * nw_ref[...] * (1.0 - lam_init)).astype(BF16)


def _diff(hb, bias_tiles, lam, nw, lam_init):
    nq = SEQ // Q_TILE
    qcol, kcol, vcol = 1536 // 128, 2048 // 128, 2560 // 128
    return pl.pallas_call(
        functools.partial(_diff_kernel, lam_init),
        grid=(BATCH, DIFF_HEADS, nq),
        in_specs=[
            pl.BlockSpec((Q_TILE, DIFF_HEAD_V), lambda b, h, i: (b * nq + i, qcol + h)),
            pl.BlockSpec((SEQ, DIFF_HEAD_V), lambda b, h, i: (b, kcol + h)),
            pl.BlockSpec((SEQ, DIFF_HEAD_V), lambda b, h, i: (b, vcol + h)),
            pl.BlockSpec((None, BIAS_SLOTS, Q_SUB, KEY_BLOCK), lambda b, h, i: (h, 0, 0, 0)),
            pl.BlockSpec((4, DIFF_HEAD_DIM), lambda b, h, i: (0, 0)),
            pl.BlockSpec((1, DIFF_HEAD_V), lambda b, h, i: (0, 0)),
        ],
        out_specs=pl.BlockSpec((Q_TILE, DIFF_HEAD_V), lambda b, h, i: (b * nq + i, h)),
        out_shape=jax.ShapeDtypeStruct((TOKENS, DIFF_WIDTH), BF16),
        scratch_shapes=[
            pltpu.VMEM((DIFF_HEAD_V, SEQ), BF16),
            pltpu.VMEM((SEQ, 2 * DIFF_HEAD_V), BF16),
        ],
        compiler_params=_params(("parallel", "parallel", "arbitrary"), V7X_VMEM_LIMIT),
        name="diff_attn",
    )(hb, hb, hb, bias_tiles, lam, nw)


OUTPROJ_TM = 512
OUTPROJ_SUB = 512


def _outproj_kernel(og_ref, od_ref, w_ref, x_ref, g_ref, b_ref, wt_ref, rb_ref,
                    out_ref, words_ref, eid_ref, wcol_ref):
    sub = OUTPROJ_SUB
    for u in range(OUTPROJ_TM // sub):
        rows = slice(u * sub, (u + 1) * sub)
        mix = _dot(og_ref[rows, :], w_ref[0:GLA_WIDTH, :]) + _dot(od_ref[rows, :], w_ref[GLA_WIDTH:, :])
        x1 = _layer_norm(ALPHA * x_ref[rows, :] + mix, g_ref[...], b_ref[...])
        out_ref[rows, :] = x1
        words_ref[u * sub * PIECES:(u + 1) * sub * PIECES, :] = _to_storage(_pack_words(x1), TOKEN_B)
        eid_ref[:, rows], wcol_ref[rows, :] = _route(x1, wt_ref, rb_ref)


def _outproj_ln(o_gla, o_diff, w_out, x2, g, b, w_router_t, router_bias_col):
    tm = OUTPROJ_TM
    r_in, r_out, r_shape = _router_specs(tm, lambda i: i)
    return pl.pallas_call(
        _outproj_kernel,
        grid=(TOKENS // tm,),
        in_specs=[
            pl.BlockSpec((tm, GLA_WIDTH), lambda i: (i, 0)),
            pl.BlockSpec((tm, DIFF_WIDTH), lambda i: (i, 0)),
            pl.BlockSpec((D_MODEL, D_MODEL), lambda i: (0, 0)),
            pl.BlockSpec((tm, D_MODEL), lambda i: (i, 0)),
            pl.BlockSpec((1, D_MODEL), lambda i: (0, 0)),
            pl.BlockSpec((1, D_MODEL), lambda i: (0, 0)),
        ] + r_in,
        out_specs=[
            pl.BlockSpec((tm, D_MODEL), lambda i: (i, 0)),
            pl.BlockSpec((tm * PIECES, PIECE), lambda i: (i, 0)),
        ] + r_out,
        out_shape=[
            jax.ShapeDtypeStruct((TOKENS, D_MODEL), F32),
            jax.ShapeDtypeStruct((TOKENS * PIECES, PIECE), U32),
        ] + r_shape,
        compiler_params=_params(("parallel",), V7X_VMEM_LIMIT),
        name="outproj_ln",
    )(o_gla, o_diff, w_out, x2, g, b, w_router_t, router_bias_col)


def _pool_kernel(x_ref, w_ref, sc_ref, g_ref, b_ref, wt_ref, rb_ref, out_ref, words_ref, eid_ref, wcol_ref):
    t = pl.program_id(1)
    r0 = pl.multiple_of(t * POOL_TILE, POOL_TILE)
    w0 = pl.multiple_of(jnp.clip(r0 - POOL_HALO, 0, SEQ - POOL_WIN), POOL_HALO)
    xt = x_ref[pl.ds(r0, POOL_TILE), :]
    xw_hi, xw_lo = _split_bf16(x_ref[pl.ds(w0, POOL_WIN), :])
    rel = ((w0 - r0) + lax.broadcasted_iota(jnp.int32, (POOL_TILE, POOL_WIN), 1)
           - lax.broadcasted_iota(jnp.int32, (POOL_TILE, POOL_WIN), 0))
    posc = r0 + lax.broadcasted_iota(jnp.int32, (POOL_TILE, 1), 0)
    ys = []
    for gi, win in enumerate(POOL_WINDOWS):
        half = win // 2
        band = jnp.where((rel >= -half) & (rel < half), 1.0, 0.0).astype(BF16)
        cols = slice(gi * POOL_GROUP_W, (gi + 1) * POOL_GROUP_W)
        wsum = _dot(band, xw_hi[:, cols]) + _dot(band, xw_lo[:, cols])
        count = (jnp.minimum(posc + half, SEQ) - jnp.maximum(posc - half, 0)).astype(F32)
        pooled = wsum / count - xt[:, cols]
        ys.append(_dot(pooled.astype(BF16), w_ref[gi]))
    mix = jnp.concatenate(ys, axis=1) * sc_ref[...]
    x1 = _layer_norm(ALPHA * xt + mix, g_ref[...], b_ref[...])
    out_ref[...] = x1
    words_ref[...] = _to_storage(_pack_words(x1), TOKEN_B)
    eid_ref[...], wcol_ref[...] = _route(x1, wt_ref, rb_ref)


def _pool_ln(x3, w_pool, scale, g, b, w_router_t, router_bias_col):
    nt = SEQ // POOL_TILE
    r_in, r_out, r_shape = _router_specs(POOL_TILE, lambda bi, t: bi * nt + t)
    return pl.pallas_call(
        _pool_kernel,
        grid=(BATCH, nt),
        in_specs=[
            pl.BlockSpec((None, SEQ, D_MODEL), lambda bi, t: (bi, 0, 0)),
            pl.BlockSpec((POOL_GROUPS, POOL_GROUP_W, POOL_GROUP_W), lambda bi, t: (0, 0, 0)),
            pl.BlockSpec((1, D_MODEL), lambda bi, t: (0, 0)),
            pl.BlockSpec((1, D_MODEL), lambda bi, t: (0, 0)),
            pl.BlockSpec((1, D_MODEL), lambda bi, t: (0, 0)),
        ] + r_in,
        out_specs=[
            pl.BlockSpec((POOL_TILE, D_MODEL), lambda bi, t: (bi * nt + t, 0)),
            pl.BlockSpec((POOL_TILE * PIECES, PIECE), lambda bi, t: (bi * nt + t, 0)),
        ] + r_out,
        out_shape=[
            jax.ShapeDtypeStruct((TOKENS, D_MODEL), F32),
            jax.ShapeDtypeStruct((TOKENS * PIECES, PIECE), U32),
        ] + r_shape,
        compiler_params=_params(("parallel", "arbitrary"), V7X_VMEM_LIMIT),
        name="pool_ln",
    )(x3, w_pool, scale, g, b, w_router_t, router_bias_col)


def _route(x1, wt_ref, rb_ref):
    x_hi, x_lo = _split_bf16(x1)
    w_hi, w_lo = _split_bf16(wt_ref[...])
    logits = _dot_nt(w_hi, x_hi) + _dot_nt(w_hi, x_lo) + _dot_nt(w_lo, x_hi)
    aff = 1.0 / (1.0 + jnp.exp(-logits))
    sel = aff + rb_ref[...]
    a = [aff[e:e + 1, :] for e in range(N_EXPERTS)]
    s = [sel[e:e + 1, :] for e in range(N_EXPERTS)]
    one = jnp.ones_like(s[0])
    zero = jnp.zeros_like(s[0])
    rank = [zero] * N_EXPERTS
    for gi in range(N_EXPERT_GROUPS):
        for i in range(EXPERTS_PER_GROUP):
            for j in range(i + 1, EXPERTS_PER_GROUP):
                ei, ej = gi * EXPERTS_PER_GROUP + i, gi * EXPERTS_PER_GROUP + j
                j_wins = s[ej] > s[ei]
                rank[ei] = rank[ei] + jnp.where(j_wins, one, zero)
                rank[ej] = rank[ej] + jnp.where(j_wins, zero, one)
    top2 = [rank[e] < 2.0 for e in range(N_EXPERTS)]
    score = []
    for gi in range(N_EXPERT_GROUPS):
        acc = zero
        for i in range(EXPERTS_PER_GROUP):
            e = gi * EXPERTS_PER_GROUP + i
            acc = acc + jnp.where(top2[e], s[e], zero)
        score.append(acc)
    best, best_g = score[0], zero
    for gi in range(1, N_EXPERT_GROUPS):
        better = score[gi] > best
        best = jnp.where(better, score[gi], best)
        best_g = jnp.where(better, float(gi) * one, best_g)
    picked = [top2[e] & (best_g == float(e // EXPERTS_PER_GROUP)) for e in range(N_EXPERTS)]
    e_lo, e_hi = 99.0 * one, -one
    for e in range(N_EXPERTS):
        e_lo = jnp.where(picked[e], jnp.minimum(e_lo, float(e)), e_lo)
        e_hi = jnp.where(picked[e], jnp.maximum(e_hi, float(e)), e_hi)
    a_lo, a_hi = zero, zero
    for e in range(N_EXPERTS):
        a_lo = jnp.where(picked[e] & (e_lo == float(e)), a[e], a_lo)
        a_hi = jnp.where(picked[e] & (e_hi == float(e)), a[e], a_hi)
    denom = a_lo + a_hi
    eid = jnp.concatenate([e_lo, e_hi], axis=0).astype(I32)
    w8 = jnp.concatenate([a_lo / denom, a_hi / denom] + [zero] * 6, axis=0)
    return eid, w8.T


def _router_specs(tm, token_block):
    const = lambda *_: (0, 0)
    ins = [pl.BlockSpec((N_EXPERTS, D_MODEL), const), pl.BlockSpec((N_EXPERTS, 1), const)]
    outs = [pl.BlockSpec((TOP_K, tm), lambda *g: (0, token_block(*g))),
            pl.BlockSpec((tm, 8), lambda *g: (token_block(*g), 0))]
    shapes = [jax.ShapeDtypeStruct((TOP_K, TOKENS), I32), jax.ShapeDtypeStruct((TOKENS, 8), F32)]
    return ins, outs, shapes


PLAN_CHUNK = 512


def _plan_kernel(eid_ref, pos_ref, meta_ref):
    n_chunks = TOKENS // PLAN_CHUNK
    erow = lax.broadcasted_iota(I32, (N_EXPERTS, PLAN_CHUNK), 0)

    def lanes(c):
        return pl.ds(pl.multiple_of(c * PLAN_CHUNK, PLAN_CHUNK), PLAN_CHUNK)

    def onehot(k, c):
        return erow == eid_ref[k:k + 1, lanes(c)]

    counts = jnp.zeros((N_EXPERTS, 1), F32)
    for k in range(TOP_K):
        counts = lax.fori_loop(
            0, n_chunks,
            lambda c, cnt, k=k: cnt + jnp.sum(jnp.where(onehot(k, c), 1.0, 0.0), axis=1, keepdims=True),
            counts)
    padded = jnp.ceil(counts * (1.0 / EXPERT_TM)) * EXPERT_TM
    ei = lax.broadcasted_iota(I32, (N_EXPERTS, N_EXPERTS), 0)
    ej = lax.broadcasted_iota(I32, (N_EXPERTS, N_EXPERTS), 1)
    padded_row = jnp.sum(jnp.where(ei == ej, padded, 0.0), axis=0, keepdims=True)
    start = jnp.sum(jnp.where(ej < ei, padded_row, 0.0), axis=1, keepdims=True)
    end = start + padded
    total = jnp.sum(padded, axis=0, keepdims=True)

    tri = jnp.where(lax.broadcasted_iota(I32, (PLAN_CHUNK, PLAN_CHUNK), 0)
                    <= lax.broadcasted_iota(I32, (PLAN_CHUNK, PLAN_CHUNK), 1), 1.0, 0.0).astype(BF16)

    def pos_body(c, carry, k):
        oh = onehot(k, c)
        ohf = jnp.where(oh, 1.0, 0.0)
        prefix = _dot(ohf.astype(BF16), tri)
        pos = jnp.sum(jnp.where(oh, start + carry + prefix - 1.0, 0.0), axis=0, keepdims=True)
        pos_ref[k:k + 1, lanes(c)] = pos.astype(I32)
        return carry + jnp.sum(ohf, axis=1, keepdims=True)

    carry = jnp.zeros((N_EXPERTS, 1), F32)
    for k in range(TOP_K):
        carry = lax.fori_loop(0, n_chunks, functools.partial(pos_body, k=k), carry)

    tile0 = lax.broadcasted_iota(I32, (1, META_LANES), 1).astype(F32) * EXPERT_TM
    n_used = total * (1.0 / EXPERT_TM)
    last = n_used - 1.0
    tile_c = jnp.minimum(tile0, last * EXPERT_TM)
    expert = jnp.sum(jnp.where(end <= tile_c, 1.0, 0.0), axis=0, keepdims=True)
    block = tile_c * (1.0 / EXPERT_TM)
    meta = jnp.concatenate([expert, block, jnp.broadcast_to(n_used, (1, META_LANES))]
                           + [jnp.zeros((1, META_LANES), F32)] * 5, axis=0)
    meta_ref[...] = meta.astype(I32)


def _plan(eid):
    return pl.pallas_call(
        _plan_kernel,
        in_specs=[pl.BlockSpec(memory_space=pltpu.VMEM)],
        out_specs=[pl.BlockSpec(memory_space=pltpu.VMEM), pl.BlockSpec(memory_space=pltpu.VMEM)],
        out_shape=[
            jax.ShapeDtypeStruct((TOP_K, TOKENS), I32),
            jax.ShapeDtypeStruct((8, META_LANES), I32),
        ],
        name="plan",
    )(eid)


def _sc_mesh():
    return plsc.VectorSubcoreMesh(core_axis_name="c", subcore_axis_name="s")


def _sc_worker_id():
    return lax.axis_index("s") * SC_CORES + lax.axis_index("c")


TOKEN_CHUNKS = TOKENS * PIECES // SC_CHUNK
CHUNKS_PER_WORKER = TOKEN_CHUNKS // SC_WORKERS


def _dispatch(words, didx):
    @functools.partial(
        pl.kernel,
        out_type=jax.ShapeDtypeStruct((SORTED_ROWS * PIECES, PIECE), U32),
        mesh=_sc_mesh(),
        scratch_types=[
            pltpu.VMEM((TOP_K, CHUNKS_PER_WORKER, SC_CHUNK), I32),
            pltpu.VMEM((DISPATCH_BUFS, SC_CHUNK, PIECE), U32),
            pltpu.SemaphoreType.DMA((DISPATCH_BUFS,)),
            pltpu.SemaphoreType.DMA((DISPATCH_BUFS, TOP_K)),
        ],
        name="moe_dispatch",
    )
    def run(words_hbm, didx_hbm, out_hbm, idx_v, bufs, rsem, wsem):
        c0 = _sc_worker_id() * CHUNKS_PER_WORKER
        for k in range(TOP_K):
            pltpu.sync_copy(didx_hbm.at[k, pl.ds(c0, CHUNKS_PER_WORKER)], idx_v.at[k])

        def read(c):
            b = c % DISPATCH_BUFS
            return pltpu.async_copy(words_hbm.at[pl.ds((c0 + c) * SC_CHUNK, SC_CHUNK)], bufs.at[b], rsem.at[b])

        def scatter(c):
            b = c % DISPATCH_BUFS
            return [pltpu.async_copy(bufs.at[b], out_hbm.at[idx_v.at[k, c]], wsem.at[b, k]) for k in range(TOP_K)]

        reads = {c: read(c) for c in range(SC_LOOKAHEAD)}
        scatters = {}
        for c in range(CHUNKS_PER_WORKER):
            if c >= DISPATCH_BUFS - SC_LOOKAHEAD:
                for d in scatters.pop(c - (DISPATCH_BUFS - SC_LOOKAHEAD)):
                    d.wait()
            if c + SC_LOOKAHEAD < CHUNKS_PER_WORKER:
                reads[c + SC_LOOKAHEAD] = read(c + SC_LOOKAHEAD)
            reads.pop(c).wait()
            scatters[c] = scatter(c)
        for ds in scatters.values():
            for d in ds:
                d.wait()

    return run(words, didx)


def _combine(sorted_out, didx):
    @functools.partial(
        pl.kernel,
        out_type=jax.ShapeDtypeStruct((TOP_K, TOKENS * PIECES, PIECE), U32),
        mesh=_sc_mesh(),
        scratch_types=[
            pltpu.VMEM((TOP_K, CHUNKS_PER_WORKER, SC_CHUNK), I32),
            pltpu.VMEM((COMBINE_BUFS, TOP_K, SC_CHUNK, PIECE), U32),
            pltpu.SemaphoreType.DMA((COMBINE_BUFS, TOP_K)),
            pltpu.SemaphoreType.DMA((COMBINE_BUFS, TOP_K)),
        ],
        name="moe_combine",
    )
    def run(src_hbm, didx_hbm, out_hbm, idx_v, bufs, rsem, wsem):
        c0 = _sc_worker_id() * CHUNKS_PER_WORKER
        for k in range(TOP_K):
            pltpu.sync_copy(didx_hbm.at[k, pl.ds(c0, CHUNKS_PER_WORKER)], idx_v.at[k])

        def gather(c):
            b = c % COMBINE_BUFS
            return [pltpu.async_copy(src_hbm.at[idx_v.at[k, c]], bufs.at[b, k], rsem.at[b, k]) for k in range(TOP_K)]

        def write(c):
            b = c % COMBINE_BUFS
            return [pltpu.async_copy(bufs.at[b, k], out_hbm.at[k, pl.ds((c0 + c) * SC_CHUNK, SC_CHUNK)],
                                     wsem.at[b, k]) for k in range(TOP_K)]

        gathers = {c: gather(c) for c in range(SC_LOOKAHEAD)}
        writes = {}
        for c in range(CHUNKS_PER_WORKER):
            if c >= COMBINE_BUFS - SC_LOOKAHEAD:
                for d in writes.pop(c - (COMBINE_BUFS - SC_LOOKAHEAD)):
                    d.wait()
            if c + SC_LOOKAHEAD < CHUNKS_PER_WORKER:
                gathers[c + SC_LOOKAHEAD] = gather(c + SC_LOOKAHEAD)
            for d in gathers.pop(c):
                d.wait()
            writes[c] = write(c)
        for ds in writes.values():
            for d in ds:
                d.wait()

    return run(sorted_out, didx)


def _expert_kernel(te_ref, tb_ref, nu_ref, xs_ref, wg_ref, wu_ref, wd_ref, out_ref, wgub, wdb):
    i = pl.program_id(0)
    fresh = (i == 0) | (te_ref[i] != te_ref[jnp.maximum(i - 1, 0)])

    @pl.when(fresh)
    def _():
        wgub[:, :D_MODEL] = wg_ref[...].astype(BF16)
        wgub[:, D_MODEL:] = wu_ref[...].astype(BF16)
        wdb[...] = wd_ref[...].astype(BF16)

    @pl.when(i < nu_ref[0])
    def _():
        lo, hi = _unpack_words(_from_storage(xs_ref[...], EXPERT_TM))
        x = jnp.concatenate([lo.astype(BF16), hi.astype(BF16)], axis=1)
        hgu = _dot(x, wgub[...])
        hg, hu = hgu[:, :D_MODEL], hgu[:, D_MODEL:]
        h = (hg / (1.0 + jnp.exp(-hg)) * hu).astype(BF16)
        y = _dot(h, wdb[...])
        out_ref[...] = _to_storage(_pack_words(y), EXPERT_TM)


def _experts(te, tb, nu, xs, w_gate, w_up, w_down, layer):
    wspec = pl.BlockSpec((None, None, D_MODEL, D_MODEL), lambda i, te, tb, nu: (layer, te[i], 0, 0))
    rows = pl.BlockSpec((EXPERT_TM * PIECES, PIECE), lambda i, te, tb, nu: (tb[i], 0))
    return pl.pallas_call(
        _expert_kernel,
        grid_spec=pltpu.PrefetchScalarGridSpec(
            num_scalar_prefetch=3,
            grid=(N_TILES,),
            in_specs=[rows, wspec, wspec, wspec],
            out_specs=rows,
            scratch_shapes=[pltpu.VMEM((D_MODEL, 2 * D_MODEL), BF16), pltpu.VMEM((D_MODEL, D_MODEL), BF16)],
        ),
        out_shape=jax.ShapeDtypeStruct((SORTED_ROWS * PIECES, PIECE), U32),
        compiler_params=_params(("arbitrary",), V7X_VMEM_LIMIT),
        name="experts",
    )(te, tb, nu, xs, w_gate, w_up, w_down)


COMBINE_TM = 512


def _combine_ln_kernel(x_ref, y_ref, w_ref, g_ref, b_ref, out_ref):
    w = w_ref[...]
    ffn = None
    for k in range(TOP_K):
        lo, hi = _unpack_words(_from_storage(y_ref[k], TOKEN_B))
        yk = w[:, k:k + 1] * jnp.concatenate([lo, hi], axis=1)
        ffn = yk if ffn is None else ffn + yk
    out_ref[...] = _layer_norm(ALPHA * x_ref[...] + ffn, g_ref[...], b_ref[...])


def _combine_ln(x1, y_tok, wcol, g, b):
    tm = COMBINE_TM
    return pl.pallas_call(
        _combine_ln_kernel,
        grid=(TOKENS // tm,),
        in_specs=[
            pl.BlockSpec((tm, D_MODEL), lambda i: (i, 0)),
            pl.BlockSpec((TOP_K, tm * PIECES, PIECE), lambda i: (0, i, 0)),
            pl.BlockSpec((tm, 8), lambda i: (i, 0)),
            pl.BlockSpec((1, D_MODEL), lambda i: (0, 0)),
            pl.BlockSpec((1, D_MODEL), lambda i: (0, 0)),
        ],
        out_specs=pl.BlockSpec((tm, D_MODEL), lambda i: (i, 0)),
        out_shape=jax.ShapeDtypeStruct((TOKENS, D_MODEL), F32),
        compiler_params=_params(("parallel",), V7X_VMEM_LIMIT),
        name="combine_ln",
    )(x1, y_tok, wcol, g, b)


def _sorted_storage_rows(pos):
    p = pos.reshape(TOP_K, TOKENS // TOKEN_B, 1, TOKEN_B)
    j = jnp.arange(PIECES, dtype=I32).reshape(1, 1, PIECES, 1)
    r = (p // EXPERT_TM) * (PIECES * EXPERT_TM) + j * EXPERT_TM + p % EXPERT_TM
    return r.reshape(TOP_K, TOKEN_CHUNKS, SC_CHUNK)


def kernel(x, rel_bias, w_in, w_gla_gate, b_gla_gate, gla_norm, diff_lambda, diff_norm, w_out_mix, w_pool,
           pool_scale, ln_g, ln_b, w_router, router_bias, w_gate, w_up, w_down):
    x2 = x.reshape(TOKENS, D_MODEL)
    bias_tiles = _bias_tiles(rel_bias)
    w_router_t = w_router.T
    router_bias_col = router_bias.reshape(N_EXPERTS, 1)
    row = lambda v: v.reshape(1, -1)
    col_scale = jnp.concatenate([
        jnp.full((GLA_KEY,), GLA_HEAD_K ** -0.5, F32), jnp.ones((GLA_KEY + 2 * GLA_WIDTH,), F32),
        jnp.full((DIFF_WIDTH,), DIFF_HEAD_DIM ** -0.5 * LOG2E, F32), jnp.ones((2 * DIFF_WIDTH,), F32)])
    for layer in range(DEPTH):
        if layer % 2 == 0:
            i = layer // 2
            wi = w_in[i]
            w_main = (jnp.concatenate([wi[:, :1536], wi[:, 1568:]], axis=1) * col_scale).astype(BF16)
            w_lr = jnp.pad(wi[:, 1536:1568], ((0, 0), (0, LR_COLS - 2 * GATE_RANK))).astype(BF16)
            hb, lr = _inproj(x2, w_main, w_lr)
            wg_pad = jnp.zeros((2, LR_COLS, GLA_KEY), F32)
            wg_pad = wg_pad.at[0, 0:GATE_RANK].set(w_gla_gate[i, 0])
            wg_pad = wg_pad.at[1, GATE_RANK:2 * GATE_RANK].set(w_gla_gate[i, 1])
            o_gla = _gla(hb, lr, wg_pad, b_gla_gate[i], row(gla_norm[i]))
            lam_init = 0.8 - 0.6 * math.exp(-0.3 * layer)
            o_diff = _diff(hb, bias_tiles, diff_lambda[i], row(diff_norm[i]), lam_init)
            x1, words, eid, wcol = _outproj_ln(o_gla, o_diff, w_out_mix[i].astype(BF16), x2,
                                               row(ln_g[layer, 0]), row(ln_b[layer, 0]),
                                               w_router_t, router_bias_col)
        else:
            j = layer // 2
            x1, words, eid, wcol = _pool_ln(x2.reshape(BATCH, SEQ, D_MODEL), w_pool[j].astype(BF16),
                                            row(pool_scale[j]), row(ln_g[layer, 0]), row(ln_b[layer, 0]),
                                            w_router_t, router_bias_col)
        pos, meta = _plan(eid)
        didx = _sorted_storage_rows(pos)
        xs = _dispatch(words, didx)
        ys = _experts(meta[0, :N_TILES], meta[1, :N_TILES], meta[2, :1], xs, w_gate, w_up, w_down, layer)
        y_tok = _combine(ys, didx)
        x2 = _combine_ln(x1, y_tok, wcol, row(ln_g[layer, 1]), row(ln_b[layer, 1]))
    return x2.reshape(BATCH, SEQ, D_MODEL)
```

```python
import functools
import math

import jax
import jax.numpy as jnp
from jax import lax
from jax.experimental import pallas as pl
from jax.experimental.pallas import tpu as pltpu
from jax.experimental.pallas import tpu_sc as plsc

F32 = jnp.float32
BF16 = jnp.bfloat16
U32 = jnp.uint32
I32 = jnp.int32

D_MODEL = 1024
BATCH = 8
SEQ = 2048
DEPTH = 4
TOKENS = BATCH * SEQ

GLA_HEADS = 4
GLA_WIDTH = 512
GLA_HEAD_V = 128
GLA_KEY = 256
GLA_HEAD_K = 64
GATE_RANK = 16
GATE_NORMALIZER = 16.0
GLA_CHUNK = 64
GLA_BLOCK = 256
CHUNKS_PER_BLOCK = GLA_BLOCK // GLA_CHUNK

DIFF_HEADS = 4
DIFF_WIDTH = 512
DIFF_HEAD_V = 128
DIFF_HEAD_DIM = 64
REL_BUCKETS = 32
Q_TILE = 1024
Q_SUB = 256
KEY_BLOCK = 128
BIAS_SLOTS = 6

POOL_WINDOWS = (2, 4, 8, 16)
POOL_GROUPS = 4
POOL_GROUP_W = 256
POOL_TILE = 512
POOL_HALO = 64
POOL_WIN = POOL_TILE + 2 * POOL_HALO

N_EXPERTS = 16
N_EXPERT_GROUPS = 4
EXPERTS_PER_GROUP = 4
TOP_K = 2

WORD_COLS = D_MODEL // 2
PIECE = 128
PIECES = WORD_COLS // PIECE
TOKEN_B = 128
EXPERT_TM = 512
PAIR_ROWS = TOKENS * TOP_K
SORTED_ROWS = PAIR_ROWS + N_EXPERTS * EXPERT_TM
N_TILES = SORTED_ROWS // EXPERT_TM
META_LANES = 128
assert N_TILES <= META_LANES

SC_CORES = 2
SC_SUBCORES = 16
SC_WORKERS = SC_CORES * SC_SUBCORES
SC_CHUNK = 128
SC_LOOKAHEAD = 2
DISPATCH_BUFS = 4
COMBINE_BUFS = 3

LOG2E = math.log2(math.e)
ALPHA = (2.0 * DEPTH) ** 0.25
LN_EPS = 1e-5
RMS_EPS = 1e-6

H_COLS = 3072
LR_COLS = 128

V7X_VMEM_LIMIT = 56 * 1024 * 1024


def _params(sem, vmem=None):
    return pltpu.CompilerParams(dimension_semantics=sem, vmem_limit_bytes=vmem)


def _split_bf16(x):
    hi = x.astype(BF16)
    lo = (x - hi.astype(F32)).astype(BF16)
    return hi, lo


def _dot(a, b):
    return jnp.dot(a, b, preferred_element_type=F32)


def _dot_nt(a, b):
    return lax.dot_general(a, b, (((1,), (1,)), ((), ())), preferred_element_type=F32)


def _dot_tn(a, b):
    return lax.dot_general(a, b, (((0,), (0,)), ((), ())), preferred_element_type=F32)


def _layer_norm(z, g, b):
    mu = jnp.mean(z, axis=-1, keepdims=True)
    zc = z - mu
    var = jnp.mean(zc * zc, axis=-1, keepdims=True)
    return zc * lax.rsqrt(var + LN_EPS) * g + b


def _pack_words(x):
    lo = lax.bitcast_convert_type(x[:, :WORD_COLS].astype(BF16).astype(F32), U32)
    hi = lax.bitcast_convert_type(x[:, WORD_COLS:].astype(BF16).astype(F32), U32)
    return (lo >> 16) | (hi & jnp.uint32(0xFFFF0000))


def _unpack_words(w):
    lo = lax.bitcast_convert_type(w << 16, F32)
    hi = lax.bitcast_convert_type(w & jnp.uint32(0xFFFF0000), F32)
    return lo, hi


def _to_storage(w, b):
    out = []
    for s in range(w.shape[0] // b):
        for j in range(PIECES):
            out.append(w[s * b:(s + 1) * b, j * PIECE:(j + 1) * PIECE])
    return jnp.concatenate(out, axis=0)


def _from_storage(r, b):
    rows = []
    for s in range(r.shape[0] // (b * PIECES)):
        base = s * b * PIECES
        rows.append(jnp.concatenate([r[base + j * b:base + (j + 1) * b, :] for j in range(PIECES)], axis=1))
    return jnp.concatenate(rows, axis=0) if len(rows) > 1 else rows[0]


INPROJ_TM = 512


def _inproj_kernel(x_ref, w_ref, wlr_ref, hb_ref, lr_ref):
    xb = x_ref[...].astype(BF16)
    hb_ref[...] = _dot(xb, w_ref[...]).astype(BF16)
    lr_ref[...] = _dot(xb, wlr_ref[...])


def _inproj(x2, w_main, w_lr):
    return pl.pallas_call(
        _inproj_kernel,
        grid=(TOKENS // INPROJ_TM,),
        in_specs=[
            pl.BlockSpec((INPROJ_TM, D_MODEL), lambda i: (i, 0)),
            pl.BlockSpec((D_MODEL, H_COLS), lambda i: (0, 0)),
            pl.BlockSpec((D_MODEL, LR_COLS), lambda i: (0, 0)),
        ],
        out_specs=[
            pl.BlockSpec((INPROJ_TM, H_COLS), lambda i: (i, 0)),
            pl.BlockSpec((INPROJ_TM, LR_COLS), lambda i: (i, 0)),
        ],
        out_shape=[
            jax.ShapeDtypeStruct((TOKENS, H_COLS), BF16),
            jax.ShapeDtypeStruct((TOKENS, LR_COLS), F32),
        ],
        compiler_params=_params(("parallel",), V7X_VMEM_LIMIT),
        name="inproj",
    )(x2, w_main, w_lr)


def _gla_kernel(q_ref, k_ref, v_ref, g_ref, lr_ref, wg_ref, bg_ref, nw_ref,
                out_ref, of_ref, ob_ref, sf_ref, sb_ref):
    blk, ch = GLA_BLOCK, GLA_CHUNK
    n_blocks = SEQ // blk
    row = lax.broadcasted_iota(jnp.int32, (blk, blk), 0)
    col = lax.broadcasted_iota(jnp.int32, (blk, blk), 1)
    same_chunk = (row // ch) == (col // ch)
    srow = lax.broadcasted_iota(jnp.int32, (GLA_WIDTH, GLA_KEY), 0) // GLA_HEAD_V
    scol = lax.broadcasted_iota(jnp.int32, (GLA_WIDTH, GLA_KEY), 1) // GLA_HEAD_K
    state_mask = srow == scol
    lane_head = lax.broadcasted_iota(jnp.int32, (blk, 2 * GLA_HEAD_K), 1) // GLA_HEAD_K

    def block_step(r0, direction, state_ref):
        fwd = direction == 0
        rows = pl.ds(r0, blk)
        lr_hi, lr_lo = _split_bf16(lr_ref[rows, :])
        w_hi, w_lo = _split_bf16(wg_ref[direction])
        logit = (_dot(lr_hi, w_hi) + _dot(lr_lo, w_hi) + _dot(lr_hi, w_lo)
                 + bg_ref[direction:direction + 1, :])
        la = (jnp.minimum(logit, 0.0) - jnp.log(1.0 + jnp.exp(-jnp.abs(logit)))) * (1.0 / GATE_NORMALIZER)
        la_hi, la_lo = _split_bf16(la)
        tri = jnp.where(same_chunk & ((col <= row) if fwd else (col >= row)), 1.0, 0.0).astype(BF16)
        b = _dot(tri, la_hi) + _dot(tri, la_lo)
        b3 = b.reshape(CHUNKS_PER_BLOCK, ch, GLA_KEY)
        mid = ch // 2 - 1 if fwd else ch // 2
        last = ch - 1 if fwd else 0
        b_mid = jnp.broadcast_to(b3[:, mid:mid + 1, :], b3.shape).reshape(blk, GLA_KEY)
        b_last = jnp.broadcast_to(b3[:, last:last + 1, :], b3.shape).reshape(blk, GLA_KEY)
        qf = q_ref[rows, :].astype(F32)
        kf = k_ref[rows, :].astype(F32)
        vb = v_ref[rows, :]
        qd = (qf * jnp.exp(b - b_mid)).astype(BF16)
        kd = (kf * jnp.exp(b_mid - b)).astype(BF16)
        qe = (qf * jnp.exp(b)).astype(BF16)
        kl = (kf * jnp.exp(b_last - b)).astype(BF16)
        keep = same_chunk & ((col <= row) if fwd else (col > row))
        intra = []
        for h in range(GLA_HEADS):
            pair = slice((h // 2) * 128, (h // 2) * 128 + 128)
            qh = jnp.where(lane_head == (h % 2), qd[:, pair], jnp.zeros_like(qd[:, pair]))
            sc = _dot_nt(qh, kd[:, pair])
            p = jnp.where(keep, sc, 0.0).astype(BF16)
            intra.append(_dot(p, vb[:, h * GLA_HEAD_V:(h + 1) * GLA_HEAD_V]))
        o_intra = jnp.concatenate(intra, axis=1)
        inter = [None] * CHUNKS_PER_BLOCK
        order = range(CHUNKS_PER_BLOCK) if fwd else range(CHUNKS_PER_BLOCK - 1, -1, -1)
        for c in order:
            cs = slice(c * ch, (c + 1) * ch)
            state = state_ref[...]
            inter[c] = _dot_nt(qe[cs, :], state.astype(BF16))
            d_state = _dot_tn(vb[cs, :], kl[cs, :])
            decay = jnp.exp(b3[c, last:last + 1, :])
            state_ref[...] = state * decay + jnp.where(state_mask, d_state, 0.0)
        return o_intra + jnp.concatenate(inter, axis=0)

    sf_ref[...] = jnp.zeros_like(sf_ref)
    sb_ref[...] = jnp.zeros_like(sb_ref)

    def scan_body(i, carry):
        rf = pl.multiple_of(i * blk, blk)
        rb = pl.multiple_of((n_blocks - 1 - i) * blk, blk)
        of_ref[pl.ds(rf, blk), :] = block_step(rf, 0, sf_ref)
        ob_ref[pl.ds(rb, blk), :] = block_step(rb, 1, sb_ref)
        return carry

    lax.fori_loop(0, n_blocks, scan_body, 0)

    def finish_body(i, carry):
        r0 = pl.multiple_of(i * blk, blk)
        o = of_ref[pl.ds(r0, blk), :] + ob_ref[pl.ds(r0, blk), :]
        gate = g_ref[pl.ds(r0, blk), :].astype(F32)
        gate = gate / (1.0 + jnp.exp(-gate))
        outs = []
        for h in range(GLA_HEADS):
            oh = o[:, h * GLA_HEAD_V:(h + 1) * GLA_HEAD_V]
            ms = jnp.mean(oh * oh, axis=-1, keepdims=True)
            outs.append(oh * lax.rsqrt(ms + RMS_EPS) * nw_ref[...])
        out_ref[pl.ds(r0, blk), :] = (jnp.concatenate(outs, axis=1) * gate).astype(BF16)
        return carry

    lax.fori_loop(0, n_blocks, finish_body, 0)


def _gla(hb, lr, wg_pad, bg, nw):
    return pl.pallas_call(
        _gla_kernel,
        grid=(BATCH,),
        in_specs=[
            pl.BlockSpec((SEQ, GLA_KEY), lambda b: (b, 0)),
            pl.BlockSpec((SEQ, GLA_KEY), lambda b: (b, 1)),
            pl.BlockSpec((SEQ, GLA_WIDTH), lambda b: (b, 1)),
            pl.BlockSpec((SEQ, GLA_WIDTH), lambda b: (b, 2)),
            pl.BlockSpec((SEQ, LR_COLS), lambda b: (b, 0)),
            pl.BlockSpec((2, LR_COLS, GLA_KEY), lambda b: (0, 0, 0)),
            pl.BlockSpec((2, GLA_KEY), lambda b: (0, 0)),
            pl.BlockSpec((1, GLA_HEAD_V), lambda b: (0, 0)),
        ],
        out_specs=pl.BlockSpec((SEQ, GLA_WIDTH), lambda b: (b, 0)),
        out_shape=jax.ShapeDtypeStruct((TOKENS, GLA_WIDTH), BF16),
        scratch_shapes=[
            pltpu.VMEM((SEQ, GLA_WIDTH), F32),
            pltpu.VMEM((SEQ, GLA_WIDTH), F32),
            pltpu.VMEM((GLA_WIDTH, GLA_KEY), F32),
            pltpu.VMEM((GLA_WIDTH, GLA_KEY), F32),
        ],
        compiler_params=_params(("parallel",), V7X_VMEM_LIMIT),
        name="gla",
    )(hb, hb, hb, hb, lr, wg_pad, bg, nw)


_BUCKET_STEPS = (12, 16, 23, 32, 46, 64, 91)


def _bias_kernel(rb_ref, out_ref):
    r = lax.broadcasted_iota(jnp.int32, (Q_SUB, KEY_BLOCK), 0)
    l = lax.broadcasted_iota(jnp.int32, (Q_SUB, KEY_BLOCK), 1)
    for s in range(BIAS_SLOTS):
        if s == 0:
            rel = jnp.full((Q_SUB, KEY_BLOCK), -SEQ, jnp.int32)
        elif s == BIAS_SLOTS - 1:
            rel = jnp.full((Q_SUB, KEY_BLOCK), SEQ, jnp.int32)
        else:
            rel = (s - 2) * KEY_BLOCK + l - r
        n = jnp.abs(rel)
        large = jnp.full_like(n, 8)
        for t in _BUCKET_STEPS:
            large = large + (n >= t).astype(jnp.int32)
        bucket = jnp.where(rel > 0, REL_BUCKETS // 2, 0) + jnp.where(n < 8, n, large)
        for h in range(DIFF_HEADS):
            acc = jnp.zeros((Q_SUB, KEY_BLOCK), F32)
            for bkt in range(REL_BUCKETS):
                acc = jnp.where(bucket == bkt, rb_ref[bkt, h], acc)
            out_ref[h, s] = acc * LOG2E


def _bias_tiles(rel_bias):
    return pl.pallas_call(
        _bias_kernel,
        in_specs=[pl.BlockSpec(memory_space=pltpu.SMEM)],
        out_specs=pl.BlockSpec(memory_space=pltpu.VMEM),
        out_shape=jax.ShapeDtypeStruct((DIFF_HEADS, BIAS_SLOTS, Q_SUB, KEY_BLOCK), F32),
        name="bias_tiles",
    )(rel_bias)


def _diff_kernel(lam_init, q_ref, k_ref, v_ref, bias_ref, lam_ref, nw_ref, out_ref, kt_ref, v1_ref, ea_ref, eb_ref):
    kt_ref[...] = k_ref[...].T
    v1_ref[:, :DIFF_HEAD_V] = v_ref[...]
    v1_ref[:, DIFF_HEAD_V:] = jnp.ones((SEQ, DIFF_HEAD_V), BF16)
    lam = lam_ref[...]
    lam_full = (jnp.exp(jnp.sum(lam[0:1] * lam[1:2], axis=-1, keepdims=True))
                - jnp.exp(jnp.sum(lam[2:3] * lam[3:4], axis=-1, keepdims=True)) + lam_init)
    lane = lax.broadcasted_iota(jnp.int32, (Q_SUB, DIFF_HEAD_V), 1)
    zero = jnp.zeros((Q_SUB, DIFF_HEAD_V), BF16)
    blocks_per_sub = Q_SUB // KEY_BLOCK
    n_sub = SEQ // Q_SUB

    def scores(u, e_ref):
        q = q_ref[pl.ds(pl.multiple_of(u * Q_SUB, Q_SUB), Q_SUB), :]
        kt = kt_ref[...]
        s = [_dot(jnp.where((lane // DIFF_HEAD_DIM) == c, q, zero), kt) for c in range(2)]
        bias = jnp.concatenate(
            [bias_ref[jnp.clip(kb - blocks_per_sub * u + 2, 0, BIAS_SLOTS - 1)]
             for kb in range(SEQ // KEY_BLOCK)], axis=1)
        for c in range(2):
            sc = s[c] + bias
            e_ref[c] = jnp.exp2(sc - jnp.max(sc, axis=-1, keepdims=True)).astype(BF16)

    def values(u, e_ref):
        ol = [_dot(e_ref[c], v1_ref[...]) for c in range(2)]
        r = [ol[c][:, :DIFF_HEAD_V] / ol[c][:, DIFF_HEAD_V:] for c in range(2)]
        o = r[0] - lam_full * r[1]
        ms = jnp.mean(o * o, axis=-1, keepdims=True)
        out_ref[pl.ds(pl.multiple_of(u * Q_SUB, Q_SUB), Q_SUB), :] = (
            o * lax.rsqrt(ms + RMS_EPS) * nw_ref[...] * (1.0 - lam_init)).astype(BF16)

    scores(0, ea_ref)

    def pair(j, carry):
        scores(2 * j + 1, eb_ref)
        values(2 * j, ea_ref)
        scores(2 * j + 2, ea_ref)
        values(2 * j + 1, eb_ref)
        return carry

    lax.fori_loop(0, n_sub // 2 - 1, pair, 0)
    scores(n_sub - 1, eb_ref)
    values(n_sub - 2, ea_ref)
    values(n_sub - 1, eb_ref)


def _diff(hb, bias_tiles, lam, nw, lam_init):
    qcol, kcol, vcol = 1536 // 128, 2048 // 128, 2560 // 128
    return pl.pallas_call(
        functools.partial(_diff_kernel, lam_init),
        grid=(BATCH, DIFF_HEADS),
        in_specs=[
            pl.BlockSpec((SEQ, DIFF_HEAD_V), lambda b, h: (b, qcol + h)),
            pl.BlockSpec((SEQ, DIFF_HEAD_V), lambda b, h: (b, kcol + h)),
            pl.BlockSpec((SEQ, DIFF_HEAD_V), lambda b, h: (b, vcol + h)),
            pl.BlockSpec((None, BIAS_SLOTS, Q_SUB, KEY_BLOCK), lambda b, h: (h, 0, 0, 0)),
            pl.BlockSpec((4, DIFF_HEAD_DIM), lambda b, h: (0, 0)),
            pl.BlockSpec((1, DIFF_HEAD_V), lambda b, h: (0, 0)),
        ],
        out_specs=pl.BlockSpec((SEQ, DIFF_HEAD_V), lambda b, h: (b, h)),
        out_shape=jax.ShapeDtypeStruct((TOKENS, DIFF_WIDTH), BF16),
        scratch_shapes=[
            pltpu.VMEM((DIFF_HEAD_V, SEQ), BF16),
            pltpu.VMEM((SEQ, 2 * DIFF_HEAD_V), BF16),
            pltpu.VMEM((2, Q_SUB, SEQ), BF16),
            pltpu.VMEM((2, Q_SUB, SEQ), BF16),
        ],
        compiler_params=_params(("parallel", "parallel"), V7X_VMEM_LIMIT),
        name="diff_attn",
    )(hb, hb, hb, bias_tiles, lam, nw)


OUTPROJ_TM = 512
OUTPROJ_SUB = 512


def _outproj_kernel(og_ref, od_ref, w_ref, x_ref, g_ref, b_ref, wt_ref, rb_ref,
                    out_ref, words_ref, eid_ref, wcol_ref):
    sub = OUTPROJ_SUB
    for u in range(OUTPROJ_TM // sub):
        rows = slice(u * sub, (u + 1) * sub)
        mix = _dot(og_ref[rows, :], w_ref[0:GLA_WIDTH, :]) + _dot(od_ref[rows, :], w_ref[GLA_WIDTH:, :])
        x1 = _layer_norm(ALPHA * x_ref[rows, :] + mix, g_ref[...], b_ref[...])
        out_ref[rows, :] = x1
        words_ref[u * sub * PIECES:(u + 1) * sub * PIECES, :] = _to_storage(_pack_words(x1), TOKEN_B)
        eid_ref[:, rows], wcol_ref[rows, :] = _route(x1, wt_ref, rb_ref)


def _outproj_ln(o_gla, o_diff, w_out, x2, g, b, w_router_t, router_bias_col):
    tm = OUTPROJ_TM
    r_in, r_out, r_shape = _router_specs(tm, lambda i: i)
    return pl.pallas_call(
        _outproj_kernel,
        grid=(TOKENS // tm,),
        in_specs=[
            pl.BlockSpec((tm, GLA_WIDTH), lambda i: (i, 0)),
            pl.BlockSpec((tm, DIFF_WIDTH), lambda i: (i, 0)),
            pl.BlockSpec((D_MODEL, D_MODEL), lambda i: (0, 0)),
            pl.BlockSpec((tm, D_MODEL), lambda i: (i, 0)),
            pl.BlockSpec((1, D_MODEL), lambda i: (0, 0)),
            pl.BlockSpec((1, D_MODEL), lambda i: (0, 0)),
        ] + r_in,
        out_specs=[
            pl.BlockSpec((tm, D_MODEL), lambda i: (i, 0)),
            pl.BlockSpec((tm * PIECES, PIECE), lambda i: (i, 0)),
        ] + r_out,
        out_shape=[
            jax.ShapeDtypeStruct((TOKENS, D_MODEL), F32),
            jax.ShapeDtypeStruct((TOKENS * PIECES, PIECE), U32),
        ] + r_shape,
        compiler_params=_params(("parallel",), V7X_VMEM_LIMIT),
        name="outproj_ln",
    )(o_gla, o_diff, w_out, x2, g, b, w_router_t, router_bias_col)


def _pool_kernel(x_ref, w_ref, sc_ref, g_ref, b_ref, wt_ref, rb_ref, out_ref, words_ref, eid_ref, wcol_ref):
    t = pl.program_id(1)
    r0 = pl.multiple_of(t * POOL_TILE, POOL_TILE)
    w0 = pl.multiple_of(jnp.clip(r0 - POOL_HALO, 0, SEQ - POOL_WIN), POOL_HALO)
    xt = x_ref[pl.ds(r0, POOL_TILE), :]
    xw_hi, xw_lo = _split_bf16(x_ref[pl.ds(w0, POOL_WIN), :])
    rel = ((w0 - r0) + lax.broadcasted_iota(jnp.int32, (POOL_TILE, POOL_WIN), 1)
           - lax.broadcasted_iota(jnp.int32, (POOL_TILE, POOL_WIN), 0))
    posc = r0 + lax.broadcasted_iota(jnp.int32, (POOL_TILE, 1), 0)
    ys = []
    for gi, win in enumerate(POOL_WINDOWS):
        half = win // 2
        band = jnp.where((rel >= -half) & (rel < half), 1.0, 0.0).astype(BF16)
        cols = slice(gi * POOL_GROUP_W, (gi + 1) * POOL_GROUP_W)
        wsum = _dot(band, xw_hi[:, cols]) + _dot(band, xw_lo[:, cols])
        count = (jnp.minimum(posc + half, SEQ) - jnp.maximum(posc - half, 0)).astype(F32)
        pooled = wsum / count - xt[:, cols]
        ys.append(_dot(pooled.astype(BF16), w_ref[gi]))
    mix = jnp.concatenate(ys, axis=1) * sc_ref[...]
    x1 = _layer_norm(ALPHA * xt + mix, g_ref[...], b_ref[...])
    out_ref[...] = x1
    words_ref[...] = _to_storage(_pack_words(x1), TOKEN_B)
    eid_ref[...], wcol_ref[...] = _route(x1, wt_ref, rb_ref)


def _pool_ln(x3, w_pool, scale, g, b, w_router_t, router_bias_col):
    nt = SEQ // POOL_TILE
    r_in, r_out, r_shape = _router_specs(POOL_TILE, lambda bi, t: bi * nt + t)
    return pl.pallas_call(
        _pool_kernel,
        grid=(BATCH, nt),
        in_specs=[
            pl.BlockSpec((None, SEQ, D_MODEL), lambda bi, t: (bi, 0, 0)),
            pl.BlockSpec((POOL_GROUPS, POOL_GROUP_W, POOL_GROUP_W), lambda bi, t: (0, 0, 0)),
            pl.BlockSpec((1, D_MODEL), lambda bi, t: (0, 0)),
            pl.BlockSpec((1, D_MODEL), lambda bi, t: (0, 0)),
            pl.BlockSpec((1, D_MODEL), lambda bi, t: (0, 0)),
        ] + r_in,
        out_specs=[
            pl.BlockSpec((POOL_TILE, D_MODEL), lambda bi, t: (bi * nt + t, 0)),
            pl.BlockSpec((POOL_TILE * PIECES, PIECE), lambda bi, t: (bi * nt + t, 0)),
        ] + r_out,
        out_shape=[
            jax.ShapeDtypeStruct((TOKENS, D_MODEL), F32),
            jax.ShapeDtypeStruct((TOKENS * PIECES, PIECE), U32),
        ] + r_shape,
        compiler_params=_params(("parallel", "arbitrary"), V7X_VMEM_LIMIT),
        name="pool_ln",
    )(x3, w_pool, scale, g, b, w_router_t, router_bias_col)


def _route(x1, wt_ref, rb_ref):
    x_hi, x_lo = _split_bf16(x1)
    w_hi, w_lo = _split_bf16(wt_ref[...])
    logits = _dot_nt(w_hi, x_hi) + _dot_nt(w_hi, x_lo) + _dot_nt(w_lo, x_hi)
    aff = 1.0 / (1.0 + jnp.exp(-logits))
    sel = aff + rb_ref[...]
    a = [aff[e:e + 1, :] for e in range(N_EXPERTS)]
    s = [sel[e:e + 1, :] for e in range(N_EXPERTS)]
    one = jnp.ones_like(s[0])
    zero = jnp.zeros_like(s[0])
    rank = [zero] * N_EXPERTS
    for gi in range(N_EXPERT_GROUPS):
        for i in range(EXPERTS_PER_GROUP):
            for j in range(i + 1, EXPERTS_PER_GROUP):
                ei, ej = gi * EXPERTS_PER_GROUP + i, gi * EXPERTS_PER_GROUP + j
                j_wins = s[ej] > s[ei]
                rank[ei] = rank[ei] + jnp.where(j_wins, one, zero)
                rank[ej] = rank[ej] + jnp.where(j_wins, zero, one)
    top2 = [rank[e] < 2.0 for e in range(N_EXPERTS)]
    score = []
    for gi in range(N_EXPERT_GROUPS):
        acc = zero
        for i in range(EXPERTS_PER_GROUP):
            e = gi * EXPERTS_PER_GROUP + i
            acc = acc + jnp.where(top2[e], s[e], zero)
        score.append(acc)
    best, best_g = score[0], zero
    for gi in range(1, N_EXPERT_GROUPS):
        better = score[gi] > best
        best = jnp.where(better, score[gi], best)
        best_g = jnp.where(better, float(gi) * one, best_g)
    picked = [top2[e] & (best_g == float(e // EXPERTS_PER_GROUP)) for e in range(N_EXPERTS)]
    e_lo, e_hi = 99.0 * one, -one
    for e in range(N_EXPERTS):
        e_lo = jnp.where(picked[e], jnp.minimum(e_lo, float(e)), e_lo)
        e_hi = jnp.where(picked[e], jnp.maximum(e_hi, float(e)), e_hi)
    a_lo, a_hi = zero, zero
    for e in range(N_EXPERTS):
        a_lo = jnp.where(picked[e] & (e_lo == float(e)), a[e], a_lo)
        a_hi = jnp.where(picked[e] & (e_hi == float(e)), a[e], a_hi)
    denom = a_lo + a_hi
    eid = jnp.concatenate([e_lo, e_hi], axis=0).astype(I32)
    w8 = jnp.concatenate([a_lo / denom, a_hi / denom] + [zero] * 6, axis=0)
    return eid, w8.T


def _router_specs(tm, token_block):
    const = lambda *_: (0, 0)
    ins = [pl.BlockSpec((N_EXPERTS, D_MODEL), const), pl.BlockSpec((N_EXPERTS, 1), const)]
    outs = [pl.BlockSpec((TOP_K, tm), lambda *g: (0, token_block(*g))),
            pl.BlockSpec((tm, 8), lambda *g: (token_block(*g), 0))]
    shapes = [jax.ShapeDtypeStruct((TOP_K, TOKENS), I32), jax.ShapeDtypeStruct((TOKENS, 8), F32)]
    return ins, outs, shapes


PLAN_CHUNK = 512


def _plan_kernel(eid_ref, pos_ref, meta_ref):
    n_chunks = TOKENS // PLAN_CHUNK
    erow = lax.broadcasted_iota(I32, (N_EXPERTS, PLAN_CHUNK), 0)

    def lanes(c):
        return pl.ds(pl.multiple_of(c * PLAN_CHUNK, PLAN_CHUNK), PLAN_CHUNK)

    def onehot(k, c):
        return erow == eid_ref[k:k + 1, lanes(c)]

    counts = jnp.zeros((N_EXPERTS, 1), F32)
    for k in range(TOP_K):
        counts = lax.fori_loop(
            0, n_chunks,
            lambda c, cnt, k=k: cnt + jnp.sum(jnp.where(onehot(k, c), 1.0, 0.0), axis=1, keepdims=True),
            counts)
    padded = jnp.ceil(counts * (1.0 / EXPERT_TM)) * EXPERT_TM
    ei = lax.broadcasted_iota(I32, (N_EXPERTS, N_EXPERTS), 0)
    ej = lax.broadcasted_iota(I32, (N_EXPERTS, N_EXPERTS), 1)
    padded_row = jnp.sum(jnp.where(ei == ej, padded, 0.0), axis=0, keepdims=True)
    start = jnp.sum(jnp.where(ej < ei, padded_row, 0.0), axis=1, keepdims=True)
    end = start + padded
    total = jnp.sum(padded, axis=0, keepdims=True)

    tri = jnp.where(lax.broadcasted_iota(I32, (PLAN_CHUNK, PLAN_CHUNK), 0)
                    <= lax.broadcasted_iota(I32, (PLAN_CHUNK, PLAN_CHUNK), 1), 1.0, 0.0).astype(BF16)

    def pos_body(c, carry, k):
        oh = onehot(k, c)
        ohf = jnp.where(oh, 1.0, 0.0)
        prefix = _dot(ohf.astype(BF16), tri)
        pos = jnp.sum(jnp.where(oh, start + carry + prefix - 1.0, 0.0), axis=0, keepdims=True)
        pos_ref[k:k + 1, lanes(c)] = pos.astype(I32)
        return carry + jnp.sum(ohf, axis=1, keepdims=True)

    carry = jnp.zeros((N_EXPERTS, 1), F32)
    for k in range(TOP_K):
        carry = lax.fori_loop(0, n_chunks, functools.partial(pos_body, k=k), carry)

    tile0 = lax.broadcasted_iota(I32, (1, META_LANES), 1).astype(F32) * EXPERT_TM
    n_used = total * (1.0 / EXPERT_TM)
    last = n_used - 1.0
    tile_c = jnp.minimum(tile0, last * EXPERT_TM)
    expert = jnp.sum(jnp.where(end <= tile_c, 1.0, 0.0), axis=0, keepdims=True)
    block = tile_c * (1.0 / EXPERT_TM)
    meta = jnp.concatenate([expert, block, jnp.broadcast_to(n_used, (1, META_LANES))]
                           + [jnp.zeros((1, META_LANES), F32)] * 5, axis=0)
    meta_ref[...] = meta.astype(I32)


def _plan(eid):
    return pl.pallas_call(
        _plan_kernel,
        in_specs=[pl.BlockSpec(memory_space=pltpu.VMEM)],
        out_specs=[pl.BlockSpec(memory_space=pltpu.VMEM), pl.BlockSpec(memory_space=pltpu.VMEM)],
        out_shape=[
            jax.ShapeDtypeStruct((TOP_K, TOKENS), I32),
            jax.ShapeDtypeStruct((8, META_LANES), I32),
        ],
        name="plan",
    )(eid)


def _sc_mesh():
    return plsc.VectorSubcoreMesh(core_axis_name="c", subcore_axis_name="s")


def _sc_worker_id():
    return lax.axis_index("s") * SC_CORES + lax.axis_index("c")


TOKEN_CHUNKS = TOKENS * PIECES // SC_CHUNK
CHUNKS_PER_WORKER = TOKEN_CHUNKS // SC_WORKERS


def _dispatch(words, didx):
    @functools.partial(
        pl.kernel,
        out_type=jax.ShapeDtypeStruct((SORTED_ROWS * PIECES, PIECE), U32),
        mesh=_sc_mesh(),
        scratch_types=[
            pltpu.VMEM((TOP_K, CHUNKS_PER_WORKER, SC_CHUNK), I32),
            pltpu.VMEM((DISPATCH_BUFS, SC_CHUNK, PIECE), U32),
            pltpu.SemaphoreType.DMA((DISPATCH_BUFS,)),
            pltpu.SemaphoreType.DMA((DISPATCH_BUFS, TOP_K)),
        ],
        name="moe_dispatch",
    )
    def run(words_hbm, didx_hbm, out_hbm, idx_v, bufs, rsem, wsem):
        c0 = _sc_worker_id() * CHUNKS_PER_WORKER
        for k in range(TOP_K):
            pltpu.sync_copy(didx_hbm.at[k, pl.ds(c0, CHUNKS_PER_WORKER)], idx_v.at[k])

        def read(c):
            b = c % DISPATCH_BUFS
            return pltpu.async_copy(words_hbm.at[pl.ds((c0 + c) * SC_CHUNK, SC_CHUNK)], bufs.at[b], rsem.at[b])

        def scatter(c):
            b = c % DISPATCH_BUFS
            return [pltpu.async_copy(bufs.at[b], out_hbm.at[idx_v.at[k, c]], wsem.at[b, k]) for k in range(TOP_K)]

        reads = {c: read(c) for c in range(SC_LOOKAHEAD)}
        scatters = {}
        for c in range(CHUNKS_PER_WORKER):
            if c >= DISPATCH_BUFS - SC_LOOKAHEAD:
                for d in scatters.pop(c - (DISPATCH_BUFS - SC_LOOKAHEAD)):
                    d.wait()
            if c + SC_LOOKAHEAD < CHUNKS_PER_WORKER:
                reads[c + SC_LOOKAHEAD] = read(c + SC_LOOKAHEAD)
            reads.pop(c).wait()
            scatters[c] = scatter(c)
        for ds in scatters.values():
            for d in ds:
                d.wait()

    return run(words, didx)


def _combine(sorted_out, didx):
    @functools.partial(
        pl.kernel,
        out_type=jax.ShapeDtypeStruct((TOP_K, TOKENS * PIECES, PIECE), U32),
        mesh=_sc_mesh(),
        scratch_types=[
            pltpu.VMEM((TOP_K, CHUNKS_PER_WORKER, SC_CHUNK), I32),
            pltpu.VMEM((COMBINE_BUFS, TOP_K, SC_CHUNK, PIECE), U32),
            pltpu.SemaphoreType.DMA((COMBINE_BUFS, TOP_K)),
            pltpu.SemaphoreType.DMA((COMBINE_BUFS, TOP_K)),
        ],
        name="moe_combine",
    )
    def run(src_hbm, didx_hbm, out_hbm, idx_v, bufs, rsem, wsem):
        c0 = _sc_worker_id() * CHUNKS_PER_WORKER
        for k in range(TOP_K):
            pltpu.sync_copy(didx_hbm.at[k, pl.ds(c0, CHUNKS_PER_WORKER)], idx_v.at[k])

        def gather(c):
            b = c % COMBINE_BUFS
            return [pltpu.async_copy(src_hbm.at[idx_v.at[k, c]], bufs.at[b, k], rsem.at[b, k]) for k in range(TOP_K)]

        def write(c):
            b = c % COMBINE_BUFS
            return [pltpu.async_copy(bufs.at[b, k], out_hbm.at[k, pl.ds((c0 + c) * SC_CHUNK, SC_CHUNK)],
                                     wsem.at[b, k]) for k in range(TOP_K)]

        gathers = {c: gather(c) for c in range(SC_LOOKAHEAD)}
        writes = {}
        for c in range(CHUNKS_PER_WORKER):
            if c >= COMBINE_BUFS - SC_LOOKAHEAD:
                for d in writes.pop(c - (COMBINE_BUFS - SC_LOOKAHEAD)):
                    d.wait()
            if c + SC_LOOKAHEAD < CHUNKS_PER_WORKER:
                gathers[c + SC_LOOKAHEAD] = gather(c + SC_LOOKAHEAD)
            for d in gathers.pop(c):
                d.wait()
            writes[c] = write(c)
        for ds in writes.values():
            for d in ds:
                d.wait()

    return run(sorted_out, didx)


def _expert_kernel(te_ref, tb_ref, nu_ref, xs_ref, wg_ref, wu_ref, wd_ref, out_ref, wgub, wdb):
    i = pl.program_id(0)
    fresh = (i == 0) | (te_ref[i] != te_ref[jnp.maximum(i - 1, 0)])

    @pl.when(fresh)
    def _():
        wgub[:, :D_MODEL] = wg_ref[...].astype(BF16)
        wgub[:, D_MODEL:] = wu_ref[...].astype(BF16)
        wdb[...] = wd_ref[...].astype(BF16)

    @pl.when(i < nu_ref[0])
    def _():
        lo, hi = _unpack_words(_from_storage(xs_ref[...], EXPERT_TM))
        x = jnp.concatenate([lo.astype(BF16), hi.astype(BF16)], axis=1)
        hgu = _dot(x, wgub[...])
        hg, hu = hgu[:, :D_MODEL], hgu[:, D_MODEL:]
        h = (hg / (1.0 + jnp.exp(-hg)) * hu).astype(BF16)
        y = _dot(h, wdb[...])
        out_ref[...] = _to_storage(_pack_words(y), EXPERT_TM)


def _experts(te, tb, nu, xs, w_gate, w_up, w_down, layer):
    wspec = pl.BlockSpec((None, None, D_MODEL, D_MODEL), lambda i, te, tb, nu: (layer, te[i], 0, 0))
    rows = pl.BlockSpec((EXPERT_TM * PIECES, PIECE), lambda i, te, tb, nu: (tb[i], 0))
    return pl.pallas_call(
        _expert_kernel,
        grid_spec=pltpu.PrefetchScalarGridSpec(
            num_scalar_prefetch=3,
            grid=(N_TILES,),
            in_specs=[rows, wspec, wspec, wspec],
            out_specs=rows,
            scratch_shapes=[pltpu.VMEM((D_MODEL, 2 * D_MODEL), BF16), pltpu.VMEM((D_MODEL, D_MODEL), BF16)],
        ),
        out_shape=jax.ShapeDtypeStruct((SORTED_ROWS * PIECES, PIECE), U32),
        compiler_params=_params(("arbitrary",), V7X_VMEM_LIMIT),
        name="experts",
    )(te, tb, nu, xs, w_gate, w_up, w_down)


COMBINE_TM = 512


def _combine_ln_kernel(x_ref, y_ref, w_ref, g_ref, b_ref, out_ref):
    w = w_ref[...]
    ffn = None
    for k in range(TOP_K):
        lo, hi = _unpack_words(_from_storage(y_ref[k], TOKEN_B))
        yk = w[:, k:k + 1] * jnp.concatenate([lo, hi], axis=1)
        ffn = yk if ffn is None else ffn + yk
    out_ref[...] = _layer_norm(ALPHA * x_ref[...] + ffn, g_ref[...], b_ref[...])


def _combine_ln(x1, y_tok, wcol, g, b):
    tm = COMBINE_TM
    return pl.pallas_call(
        _combine_ln_kernel,
        grid=(TOKENS // tm,),
        in_specs=[
            pl.BlockSpec((tm, D_MODEL), lambda i: (i, 0)),
            pl.BlockSpec((TOP_K, tm * PIECES, PIECE), lambda i: (0, i, 0)),
            pl.BlockSpec((tm, 8), lambda i: (i, 0)),
            pl.BlockSpec((1, D_MODEL), lambda i: (0, 0)),
            pl.BlockSpec((1, D_MODEL), lambda i: (0, 0)),
        ],
        out_specs=pl.BlockSpec((tm, D_MODEL), lambda i: (i, 0)),
        out_shape=jax.ShapeDtypeStruct((TOKENS, D_MODEL), F32),
        compiler_params=_params(("parallel",), V7X_VMEM_LIMIT),
        name="combine_ln",
    )(x1, y_tok, wcol, g, b)


def _sorted_storage_rows(pos):
    p = pos.reshape(TOP_K, TOKENS // TOKEN_B, 1, TOKEN_B)
    j = jnp.arange(PIECES, dtype=I32).reshape(1, 1, PIECES, 1)
    r = (p // EXPERT_TM) * (PIECES * EXPERT_TM) + j * EXPERT_TM + p % EXPERT_TM
    return r.reshape(TOP_K, TOKEN_CHUNKS, SC_CHUNK)


def kernel(x, rel_bias, w_in, w_gla_gate, b_gla_gate, gla_norm, diff_lambda, diff_norm, w_out_mix, w_pool,
           pool_scale, ln_g, ln_b, w_router, router_bias, w_gate, w_up, w_down):
    x2 = x.reshape(TOKENS, D_MODEL)
    bias_tiles = _bias_tiles(rel_bias)
    w_router_t = w_router.T
    router_bias_col = router_bias.reshape(N_EXPERTS, 1)
    row = lambda v: v.reshape(1, -1)
    col_scale = jnp.concatenate([
        jnp.full((GLA_KEY,), GLA_HEAD_K ** -0.5, F32), jnp.ones((GLA_KEY + 2 * GLA_WIDTH,), F32),
        jnp.full((DIFF_WIDTH,), DIFF_HEAD_DIM ** -0.5 * LOG2E, F32), jnp.ones((2 * DIFF_WIDTH,), F32)])
    for layer in range(DEPTH):
        if layer % 2 == 0:
            i = layer // 2
            wi = w_in[i]
            w_main = (jnp.concatenate([wi[:, :1536], wi[:, 1568:]], axis=1) * col_scale).astype(BF16)
            w_lr = jnp.pad(wi[:, 1536:1568], ((0, 0), (0, LR_COLS - 2 * GATE_RANK))).astype(BF16)
            hb, lr = _inproj(x2, w_main, w_lr)
            wg_pad = jnp.zeros((2, LR_COLS, GLA_KEY), F32)
            wg_pad = wg_pad.at[0, 0:GATE_RANK].set(w_gla_gate[i, 0])
            wg_pad = wg_pad.at[1, GATE_RANK:2 * GATE_RANK].set(w_gla_gate[i, 1])
            o_gla = _gla(hb, lr, wg_pad, b_gla_gate[i], row(gla_norm[i]))
            lam_init = 0.8 - 0.6 * math.exp(-0.3 * layer)
            o_diff = _diff(hb, bias_tiles, diff_lambda[i], row(diff_norm[i]), lam_init)
            x1, words, eid, wcol = _outproj_ln(o_gla, o_diff, w_out_mix[i].astype(BF16), x2,
                                               row(ln_g[layer, 0]), row(ln_b[layer, 0]),
                                               w_router_t, router_bias_col)
        else:
            j = layer // 2
            x1, words, eid, wcol = _pool_ln(x2.reshape(BATCH, SEQ, D_MODEL), w_pool[j].astype(BF16),
                                            row(pool_scale[j]), row(ln_g[layer, 0]), row(ln_b[layer, 0]),
                                            w_router_t, router_bias_col)
        pos, meta = _plan(eid)
        didx = _sorted_storage_rows(pos)
        xs = _dispatch(words, didx)
        ys = _experts(meta[0, :N_TILES], meta[1, :N_TILES], meta[2, :1], xs, w_gate, w_up, w_down, layer)
        y_tok = _combine(ys, didx)
        x2 = _combine_ln(x1, y_tok, wcol, row(ln_g[layer, 1]), row(ln_b[layer, 1]))
    return x2.reshape(BATCH, SEQ, D_MODEL)
```

```python
import functools
import math

import jax
import jax.numpy as jnp
from jax import lax
from jax.experimental import pallas as pl
from jax.experimental.pallas import tpu as pltpu
from jax.experimental.pallas import tpu_sc as plsc

F32 = jnp.float32
BF16 = jnp.bfloat16
U32 = jnp.uint32
I32 = jnp.int32

D_MODEL = 1024
BATCH = 8
SEQ = 2048
DEPTH = 4
TOKENS = BATCH * SEQ

GLA_HEADS = 4
GLA_WIDTH = 512
GLA_HEAD_V = 128
GLA_KEY = 256
GLA_HEAD_K = 64
GATE_RANK = 16
GATE_NORMALIZER = 16.0
GLA_CHUNK = 64
GLA_BLOCK = 256
CHUNKS_PER_BLOCK = GLA_BLOCK // GLA_CHUNK

DIFF_HEADS = 4
DIFF_WIDTH = 512
DIFF_HEAD_V = 128
DIFF_HEAD_DIM = 64
REL_BUCKETS = 32
Q_TILE = 1024
Q_SUB = 256
KEY_BLOCK = 128
BIAS_SLOTS = 6

POOL_WINDOWS = (2, 4, 8, 16)
POOL_GROUPS = 4
POOL_GROUP_W = 256
POOL_TILE = 512
POOL_HALO = 64
POOL_WIN = POOL_TILE + 2 * POOL_HALO

N_EXPERTS = 16
N_EXPERT_GROUPS = 4
EXPERTS_PER_GROUP = 4
TOP_K = 2

WORD_COLS = D_MODEL // 2
PIECE = 128
PIECES = WORD_COLS // PIECE
TOKEN_B = 128
EXPERT_TM = 512
PAIR_ROWS = TOKENS * TOP_K
SORTED_ROWS = PAIR_ROWS + N_EXPERTS * EXPERT_TM
N_TILES = SORTED_ROWS // EXPERT_TM
META_LANES = 128
assert N_TILES <= META_LANES

SC_CORES = 2
SC_SUBCORES = 16
SC_WORKERS = SC_CORES * SC_SUBCORES
SC_CHUNK = 128
SC_LOOKAHEAD = 2
DISPATCH_BUFS = 4
COMBINE_BUFS = 3

LOG2E = math.log2(math.e)
ALPHA = (2.0 * DEPTH) ** 0.25
LN_EPS = 1e-5
RMS_EPS = 1e-6

H_COLS = 3072
LR_COLS = 128

V7X_VMEM_LIMIT = 56 * 1024 * 1024


def _params(sem, vmem=None):
    return pltpu.CompilerParams(dimension_semantics=sem, vmem_limit_bytes=vmem)


def _split_bf16(x):
    hi = x.astype(BF16)
    lo = (x - hi.astype(F32)).astype(BF16)
    return hi, lo


def _dot(a, b):
    return jnp.dot(a, b, preferred_element_type=F32)


def _dot_nt(a, b):
    return lax.dot_general(a, b, (((1,), (1,)), ((), ())), preferred_element_type=F32)


def _dot_tn(a, b):
    return lax.dot_general(a, b, (((0,), (0,)), ((), ())), preferred_element_type=F32)


def _layer_norm(z, g, b):
    mu = jnp.mean(z, axis=-1, keepdims=True)
    zc = z - mu
    var = jnp.mean(zc * zc, axis=-1, keepdims=True)
    return zc * lax.rsqrt(var + LN_EPS) * g + b


def _pack_words(x):
    lo = lax.bitcast_convert_type(x[:, :WORD_COLS].astype(BF16).astype(F32), U32)
    hi = lax.bitcast_convert_type(x[:, WORD_COLS:].astype(BF16).astype(F32), U32)
    return (lo >> 16) | (hi & jnp.uint32(0xFFFF0000))


def _unpack_words(w):
    lo = lax.bitcast_convert_type(w << 16, F32)
    hi = lax.bitcast_convert_type(w & jnp.uint32(0xFFFF0000), F32)
    return lo, hi


def _to_storage(w, b):
    out = []
    for s in range(w.shape[0] // b):
        for j in range(PIECES):
            out.append(w[s * b:(s + 1) * b, j * PIECE:(j + 1) * PIECE])
    return jnp.concatenate(out, axis=0)


def _from_storage(r, b):
    rows = []
    for s in range(r.shape[0] // (b * PIECES)):
        base = s * b * PIECES
        rows.append(jnp.concatenate([r[base + j * b:base + (j + 1) * b, :] for j in range(PIECES)], axis=1))
    return jnp.concatenate(rows, axis=0) if len(rows) > 1 else rows[0]


INPROJ_TM = 512


def _inproj_kernel(x_ref, w_ref, wlr_ref, hb_ref, lr_ref):
    xb = x_ref[...].astype(BF16)
    hb_ref[...] = _dot(xb, w_ref[...]).astype(BF16)
    lr_ref[...] = _dot(xb, wlr_ref[...])


def _inproj(x2, w_main, w_lr):
    return pl.pallas_call(
        _inproj_kernel,
        grid=(TOKENS // INPROJ_TM,),
        in_specs=[
            pl.BlockSpec((INPROJ_TM, D_MODEL), lambda i: (i, 0)),
            pl.BlockSpec((D_MODEL, H_COLS), lambda i: (0, 0)),
            pl.BlockSpec((D_MODEL, LR_COLS), lambda i: (0, 0)),
        ],
        out_specs=[
            pl.BlockSpec((INPROJ_TM, H_COLS), lambda i: (i, 0)),
            pl.BlockSpec((INPROJ_TM, LR_COLS), lambda i: (i, 0)),
        ],
        out_shape=[
            jax.ShapeDtypeStruct((TOKENS, H_COLS), BF16),
            jax.ShapeDtypeStruct((TOKENS, LR_COLS), F32),
        ],
        compiler_params=_params(("parallel",), V7X_VMEM_LIMIT),
        name="inproj",
    )(x2, w_main, w_lr)


def _gla_kernel(q_ref, k_ref, v_ref, g_ref, lr_ref, wg_ref, bg_ref, nw_ref,
                out_ref, of_ref, ob_ref, sf_ref, sb_ref):
    blk, ch = GLA_BLOCK, GLA_CHUNK
    n_blocks = SEQ // blk
    row = lax.broadcasted_iota(jnp.int32, (blk, blk), 0)
    col = lax.broadcasted_iota(jnp.int32, (blk, blk), 1)
    same_chunk = (row // ch) == (col // ch)
    srow = lax.broadcasted_iota(jnp.int32, (GLA_WIDTH, GLA_KEY), 0) // GLA_HEAD_V
    scol = lax.broadcasted_iota(jnp.int32, (GLA_WIDTH, GLA_KEY), 1) // GLA_HEAD_K
    state_mask = srow == scol
    lane_head = lax.broadcasted_iota(jnp.int32, (blk, 2 * GLA_HEAD_K), 1) // GLA_HEAD_K

    def block_step(r0, direction, state_ref):
        fwd = direction == 0
        rows = pl.ds(r0, blk)
        lr_hi, lr_lo = _split_bf16(lr_ref[rows, :])
        w_hi, w_lo = _split_bf16(wg_ref[direction])
        logit = (_dot(lr_hi, w_hi) + _dot(lr_lo, w_hi) + _dot(lr_hi, w_lo)
                 + bg_ref[direction:direction + 1, :])
        la = (jnp.minimum(logit, 0.0) - jnp.log(1.0 + jnp.exp(-jnp.abs(logit)))) * (1.0 / GATE_NORMALIZER)
        la_hi, la_lo = _split_bf16(la)
        tri = jnp.where(same_chunk & ((col <= row) if fwd else (col >= row)), 1.0, 0.0).astype(BF16)
        b = _dot(tri, la_hi) + _dot(tri, la_lo)
        b3 = b.reshape(CHUNKS_PER_BLOCK, ch, GLA_KEY)
        mid = ch // 2 - 1 if fwd else ch // 2
        last = ch - 1 if fwd else 0
        b_mid = jnp.broadcast_to(b3[:, mid:mid + 1, :], b3.shape).reshape(blk, GLA_KEY)
        b_last = jnp.broadcast_to(b3[:, last:last + 1, :], b3.shape).reshape(blk, GLA_KEY)
        qf = q_ref[rows, :].astype(F32)
        kf = k_ref[rows, :].astype(F32)
        vb = v_ref[rows, :]
        qd = (qf * jnp.exp(b - b_mid)).astype(BF16)
        kd = (kf * jnp.exp(b_mid - b)).astype(BF16)
        qe = (qf * jnp.exp(b)).astype(BF16)
        kl = (kf * jnp.exp(b_last - b)).astype(BF16)
        keep = same_chunk & ((col <= row) if fwd else (col > row))
        intra = []
        for h in range(GLA_HEADS):
            pair = slice((h // 2) * 128, (h // 2) * 128 + 128)
            qh = jnp.where(lane_head == (h % 2), qd[:, pair], jnp.zeros_like(qd[:, pair]))
            sc = _dot_nt(qh, kd[:, pair])
            p = jnp.where(keep, sc, 0.0).astype(BF16)
            intra.append(_dot(p, vb[:, h * GLA_HEAD_V:(h + 1) * GLA_HEAD_V]))
        o_intra = jnp.concatenate(intra, axis=1)
        inter = [None] * CHUNKS_PER_BLOCK
        order = range(CHUNKS_PER_BLOCK) if fwd else range(CHUNKS_PER_BLOCK - 1, -1, -1)
        for c in order:
            cs = slice(c * ch, (c + 1) * ch)
            state = state_ref[...]
            inter[c] = _dot_nt(qe[cs, :], state.astype(BF16))
            d_state = _dot_tn(vb[cs, :], kl[cs, :])
            decay = jnp.exp(b3[c, last:last + 1, :])
            state_ref[...] = state * decay + jnp.where(state_mask, d_state, 0.0)
        return o_intra + jnp.concatenate(inter, axis=0)

    sf_ref[...] = jnp.zeros_like(sf_ref)
    sb_ref[...] = jnp.zeros_like(sb_ref)

    def scan_body(i, carry):
        rf = pl.multiple_of(i * blk, blk)
        rb = pl.multiple_of((n_blocks - 1 - i) * blk, blk)
        of_ref[pl.ds(rf, blk), :] = block_step(rf, 0, sf_ref)
        ob_ref[pl.ds(rb, blk), :] = block_step(rb, 1, sb_ref)
        return carry

    lax.fori_loop(0, n_blocks, scan_body, 0)

    def finish_body(i, carry):
        r0 = pl.multiple_of(i * blk, blk)
        o = of_ref[pl.ds(r0, blk), :] + ob_ref[pl.ds(r0, blk), :]
        gate = g_ref[pl.ds(r0, blk), :].astype(F32)
        gate = gate / (1.0 + jnp.exp(-gate))
        outs = []
        for h in range(GLA_HEADS):
            oh = o[:, h * GLA_HEAD_V:(h + 1) * GLA_HEAD_V]
            ms = jnp.mean(oh * oh, axis=-1, keepdims=True)
            outs.append(oh * lax.rsqrt(ms + RMS_EPS) * nw_ref[...])
        out_ref[pl.ds(r0, blk), :] = (jnp.concatenate(outs, axis=1) * gate).astype(BF16)
        return carry

    lax.fori_loop(0, n_blocks, finish_body, 0)


def _gla(hb, lr, wg_pad, bg, nw):
    return pl.pallas_call(
        _gla_kernel,
        grid=(BATCH,),
        in_specs=[
            pl.BlockSpec((SEQ, GLA_KEY), lambda b: (b, 0)),
            pl.BlockSpec((SEQ, GLA_KEY), lambda b: (b, 1)),
            pl.BlockSpec((SEQ, GLA_WIDTH), lambda b: (b, 1)),
            pl.BlockSpec((SEQ, GLA_WIDTH), lambda b: (b, 2)),
            pl.BlockSpec((SEQ, LR_COLS), lambda b: (b, 0)),
            pl.BlockSpec((2, LR_COLS, GLA_KEY), lambda b: (0, 0, 0)),
            pl.BlockSpec((2, GLA_KEY), lambda b: (0, 0)),
            pl.BlockSpec((1, GLA_HEAD_V), lambda b: (0, 0)),
        ],
        out_specs=pl.BlockSpec((SEQ, GLA_WIDTH), lambda b: (b, 0)),
        out_shape=jax.ShapeDtypeStruct((TOKENS, GLA_WIDTH), BF16),
        scratch_shapes=[
            pltpu.VMEM((SEQ, GLA_WIDTH), F32),
            pltpu.VMEM((SEQ, GLA_WIDTH), F32),
            pltpu.VMEM((GLA_WIDTH, GLA_KEY), F32),
            pltpu.VMEM((GLA_WIDTH, GLA_KEY), F32),
        ],
        compiler_params=_params(("parallel",), V7X_VMEM_LIMIT),
        name="gla",
    )(hb, hb, hb, hb, lr, wg_pad, bg, nw)


_BUCKET_STEPS = (12, 16, 23, 32, 46, 64, 91)


def _bias_kernel(rb_ref, out_ref):
    r = lax.broadcasted_iota(jnp.int32, (Q_SUB, KEY_BLOCK), 0)
    l = lax.broadcasted_iota(jnp.int32, (Q_SUB, KEY_BLOCK), 1)
    for s in range(BIAS_SLOTS):
        if s == 0:
            rel = jnp.full((Q_SUB, KEY_BLOCK), -SEQ, jnp.int32)
        elif s == BIAS_SLOTS - 1:
            rel = jnp.full((Q_SUB, KEY_BLOCK), SEQ, jnp.int32)
        else:
            rel = (s - 2) * KEY_BLOCK + l - r
        n = jnp.abs(rel)
        large = jnp.full_like(n, 8)
        for t in _BUCKET_STEPS:
            large = large + (n >= t).astype(jnp.int32)
        bucket = jnp.where(rel > 0, REL_BUCKETS // 2, 0) + jnp.where(n < 8, n, large)
        for h in range(DIFF_HEADS):
            acc = jnp.zeros((Q_SUB, KEY_BLOCK), F32)
            for bkt in range(REL_BUCKETS):
                acc = jnp.where(bucket == bkt, rb_ref[bkt, h], acc)
            out_ref[h, s] = acc * LOG2E


def _bias_tiles(rel_bias):
    return pl.pallas_call(
        _bias_kernel,
        in_specs=[pl.BlockSpec(memory_space=pltpu.SMEM)],
        out_specs=pl.BlockSpec(memory_space=pltpu.VMEM),
        out_shape=jax.ShapeDtypeStruct((DIFF_HEADS, BIAS_SLOTS, Q_SUB, KEY_BLOCK), F32),
        name="bias_tiles",
    )(rel_bias)


def _diff_kernel(lam_init, q_ref, k_ref, v_ref, bias_ref, lam_ref, nw_ref, out_ref, kt_ref, v1_ref, ea_ref, eb_ref):
    kt_ref[...] = k_ref[...].T
    v1_ref[:, :DIFF_HEAD_V] = v_ref[...]
    v1_ref[:, DIFF_HEAD_V:] = jnp.ones((SEQ, DIFF_HEAD_V), BF16)
    lam = lam_ref[...]
    lam_full = (jnp.exp(jnp.sum(lam[0:1] * lam[1:2], axis=-1, keepdims=True))
                - jnp.exp(jnp.sum(lam[2:3] * lam[3:4], axis=-1, keepdims=True)) + lam_init)
    lane = lax.broadcasted_iota(jnp.int32, (Q_SUB, DIFF_HEAD_V), 1)
    zero = jnp.zeros((Q_SUB, DIFF_HEAD_V), BF16)
    blocks_per_sub = Q_SUB // KEY_BLOCK
    n_sub = SEQ // Q_SUB

    def scores(u, e_ref):
        q = q_ref[pl.ds(pl.multiple_of(u * Q_SUB, Q_SUB), Q_SUB), :]
        kt = kt_ref[...]
        s = [_dot(jnp.where((lane // DIFF_HEAD_DIM) == c, q, zero), kt) for c in range(2)]
        bias = jnp.concatenate(
            [bias_ref[jnp.clip(kb - blocks_per_sub * u + 2, 0, BIAS_SLOTS - 1)]
             for kb in range(SEQ // KEY_BLOCK)], axis=1)
        for c in range(2):
            sc = s[c] + bias
            e_ref[c] = jnp.exp2(sc - jnp.max(sc, axis=-1, keepdims=True)).astype(BF16)

    def values(u, e_ref):
        ol = [_dot(e_ref[c], v1_ref[...]) for c in range(2)]
        r = [ol[c][:, :DIFF_HEAD_V] / ol[c][:, DIFF_HEAD_V:] for c in range(2)]
        o = r[0] - lam_full * r[1]
        ms = jnp.mean(o * o, axis=-1, keepdims=True)
        out_ref[pl.ds(pl.multiple_of(u * Q_SUB, Q_SUB), Q_SUB), :] = (
            o * lax.rsqrt(ms + RMS_EPS) * nw_ref[...] * (1.0 - lam_init)).astype(BF16)

    scores(0, ea_ref)

    def pair(j, carry):
        scores(2 * j + 1, eb_ref)
        values(2 * j, ea_ref)
        scores(2 * j + 2, ea_ref)
        values(2 * j + 1, eb_ref)
        return carry

    lax.fori_loop(0, n_sub // 2 - 1, pair, 0)
    scores(n_sub - 1, eb_ref)
    values(n_sub - 2, ea_ref)
    values(n_sub - 1, eb_ref)


def _diff(hb, bias_tiles, lam, nw, lam_init):
    qcol, kcol, vcol = 1536 // 128, 2048 // 128, 2560 // 128
    return pl.pallas_call(
        functools.partial(_diff_kernel, lam_init),
        grid=(BATCH, DIFF_HEADS),
        in_specs=[
            pl.BlockSpec((SEQ, DIFF_HEAD_V), lambda b, h: (b, qcol + h)),
            pl.BlockSpec((SEQ, DIFF_HEAD_V), lambda b, h: (b, kcol + h)),
            pl.BlockSpec((SEQ, DIFF_HEAD_V), lambda b, h: (b, vcol + h)),
            pl.BlockSpec((None, BIAS_SLOTS, Q_SUB, KEY_BLOCK), lambda b, h: (h, 0, 0, 0)),
            pl.BlockSpec((4, DIFF_HEAD_DIM), lambda b, h: (0, 0)),
            pl.BlockSpec((1, DIFF_HEAD_V), lambda b, h: (0, 0)),
        ],
        out_specs=pl.BlockSpec((SEQ, DIFF_HEAD_V), lambda b, h: (b, h)),
        out_shape=jax.ShapeDtypeStruct((TOKENS, DIFF_WIDTH), BF16),
        scratch_shapes=[
            pltpu.VMEM((DIFF_HEAD_V, SEQ), BF16),
            pltpu.VMEM((SEQ, 2 * DIFF_HEAD_V), BF16),
            pltpu.VMEM((2, Q_SUB, SEQ), BF16),
            pltpu.VMEM((2, Q_SUB, SEQ), BF16),
        ],
        compiler_params=_params(("parallel", "parallel"), V7X_VMEM_LIMIT),
        name="diff_attn",
    )(hb, hb, hb, bias_tiles, lam, nw)


OUTPROJ_TM = 512
OUTPROJ_SUB = 512


def _outproj_kernel(og_ref, od_ref, w_ref, x_ref, g_ref, b_ref, wt_ref, rb_ref,
                    out_ref, words_ref, eid_ref, wcol_ref):
    sub = OUTPROJ_SUB
    for u in range(OUTPROJ_TM // sub):
        rows = slice(u * sub, (u + 1) * sub)
        mix = _dot(og_ref[rows, :], w_ref[0:GLA_WIDTH, :]) + _dot(od_ref[rows, :], w_ref[GLA_WIDTH:, :])
        x1 = _layer_norm(ALPHA * x_ref[rows, :] + mix, g_ref[...], b_ref[...])
        out_ref[rows, :] = x1
        words_ref[u * sub * PIECES:(u + 1) * sub * PIECES, :] = _to_storage(_pack_words(x1), TOKEN_B)
        eid_ref[:, rows], wcol_ref[rows, :] = _route(x1, wt_ref, rb_ref)


def _outproj_ln(o_gla, o_diff, w_out, x2, g, b, w_router_t, router_bias_col):
    tm = OUTPROJ_TM
    r_in, r_out, r_shape = _router_specs(tm, lambda i: i)
    return pl.pallas_call(
        _outproj_kernel,
        grid=(TOKENS // tm,),
        in_specs=[
            pl.BlockSpec((tm, GLA_WIDTH), lambda i: (i, 0)),
            pl.BlockSpec((tm, DIFF_WIDTH), lambda i: (i, 0)),
            pl.BlockSpec((D_MODEL, D_MODEL), lambda i: (0, 0)),
            pl.BlockSpec((tm, D_MODEL), lambda i: (i, 0)),
            pl.BlockSpec((1, D_MODEL), lambda i: (0, 0)),
            pl.BlockSpec((1, D_MODEL), lambda i: (0, 0)),
        ] + r_in,
        out_specs=[
            pl.BlockSpec((tm, D_MODEL), lambda i: (i, 0)),
            pl.BlockSpec((tm * PIECES, PIECE), lambda i: (i, 0)),
        ] + r_out,
        out_shape=[
            jax.ShapeDtypeStruct((TOKENS, D_MODEL), F32),
            jax.ShapeDtypeStruct((TOKENS * PIECES, PIECE), U32),
        ] + r_shape,
        compiler_params=_params(("parallel",), V7X_VMEM_LIMIT),
        name="outproj_ln",
    )(o_gla, o_diff, w_out, x2, g, b, w_router_t, router_bias_col)


def _pool_kernel(x_ref, w_ref, sc_ref, g_ref, b_ref, wt_ref, rb_ref, out_ref, words_ref, eid_ref, wcol_ref):
    t = pl.program_id(1)
    r0 = pl.multiple_of(t * POOL_TILE, POOL_TILE)
    w0 = pl.multiple_of(jnp.clip(r0 - POOL_HALO, 0, SEQ - POOL_WIN), POOL_HALO)
    xt = x_ref[pl.ds(r0, POOL_TILE), :]
    xw_hi, xw_lo = _split_bf16(x_ref[pl.ds(w0, POOL_WIN), :])
    rel = ((w0 - r0) + lax.broadcasted_iota(jnp.int32, (POOL_TILE, POOL_WIN), 1)
           - lax.broadcasted_iota(jnp.int32, (POOL_TILE, POOL_WIN), 0))
    posc = r0 + lax.broadcasted_iota(jnp.int32, (POOL_TILE, 1), 0)
    ys = []
    for gi, win in enumerate(POOL_WINDOWS):
        half = win // 2
        band = jnp.where((rel >= -half) & (rel < half), 1.0, 0.0).astype(BF16)
        cols = slice(gi * POOL_GROUP_W, (gi + 1) * POOL_GROUP_W)
        wsum = _dot(band, xw_hi[:, cols]) + _dot(band, xw_lo[:, cols])
        count = (jnp.minimum(posc + half, SEQ) - jnp.maximum(posc - half, 0)).astype(F32)
        pooled = wsum / count - xt[:, cols]
        ys.append(_dot(pooled.astype(BF16), w_ref[gi]))
    mix = jnp.concatenate(ys, axis=1) * sc_ref[...]
    x1 = _layer_norm(ALPHA * xt + mix, g_ref[...], b_ref[...])
    out_ref[...] = x1
    words_ref[...] = _to_storage(_pack_words(x1), TOKEN_B)
    eid_ref[...], wcol_ref[...] = _route(x1, wt_ref, rb_ref)


def _pool_ln(x3, w_pool, scale, g, b, w_router_t, router_bias_col):
    nt = SEQ // POOL_TILE
    r_in, r_out, r_shape = _router_specs(POOL_TILE, lambda bi, t: bi * nt + t)
    return pl.pallas_call(
        _pool_kernel,
        grid=(BATCH, nt),
        in_specs=[
            pl.BlockSpec((None, SEQ, D_MODEL), lambda bi, t: (bi, 0, 0)),
            pl.BlockSpec((POOL_GROUPS, POOL_GROUP_W, POOL_GROUP_W), lambda bi, t: (0, 0, 0)),
            pl.BlockSpec((1, D_MODEL), lambda bi, t: (0, 0)),
            pl.BlockSpec((1, D_MODEL), lambda bi, t: (0, 0)),
            pl.BlockSpec((1, D_MODEL), lambda bi, t: (0, 0)),
        ] + r_in,
        out_specs=[
            pl.BlockSpec((POOL_TILE, D_MODEL), lambda bi, t: (bi * nt + t, 0)),
            pl.BlockSpec((POOL_TILE * PIECES, PIECE), lambda bi, t: (bi * nt + t, 0)),
        ] + r_out,
        out_shape=[
            jax.ShapeDtypeStruct((TOKENS, D_MODEL), F32),
            jax.ShapeDtypeStruct((TOKENS * PIECES, PIECE), U32),
        ] + r_shape,
        compiler_params=_params(("parallel", "arbitrary"), V7X_VMEM_LIMIT),
        name="pool_ln",
    )(x3, w_pool, scale, g, b, w_router_t, router_bias_col)


def _route(x1, wt_ref, rb_ref):
    x_hi, x_lo = _split_bf16(x1)
    w_hi, w_lo = _split_bf16(wt_ref[...])
    logits = _dot_nt(w_hi, x_hi) + _dot_nt(w_hi, x_lo) + _dot_nt(w_lo, x_hi)
    aff = 1.0 / (1.0 + jnp.exp(-logits))
    sel = aff + rb_ref[...]
    a = [aff[e:e + 1, :] for e in range(N_EXPERTS)]
    s = [sel[e:e + 1, :] for e in range(N_EXPERTS)]
    one = jnp.ones_like(s[0])
    zero = jnp.zeros_like(s[0])
    rank = [zero] * N_EXPERTS
    for gi in range(N_EXPERT_GROUPS):
        for i in range(EXPERTS_PER_GROUP):
            for j in range(i + 1, EXPERTS_PER_GROUP):
                ei, ej = gi * EXPERTS_PER_GROUP + i, gi * EXPERTS_PER_GROUP + j
                j_wins = s[ej] > s[ei]
                rank[ei] = rank[ei] + jnp.where(j_wins, one, zero)
                rank[ej] = rank[ej] + jnp.where(j_wins, zero, one)
    top2 = [rank[e] < 2.0 for e in range(N_EXPERTS)]
    score = []
    for gi in range(N_EXPERT_GROUPS):
        acc = zero
        for i in range(EXPERTS_PER_GROUP):
            e = gi * EXPERTS_PER_GROUP + i
            acc = acc + jnp.where(top2[e], s[e], zero)
        score.append(acc)
    best, best_g = score[0], zero
    for gi in range(1, N_EXPERT_GROUPS):
        better = score[gi] > best
        best = jnp.where(better, score[gi], best)
        best_g = jnp.where(better, float(gi) * one, best_g)
    picked = [top2[e] & (best_g == float(e // EXPERTS_PER_GROUP)) for e in range(N_EXPERTS)]
    e_lo, e_hi = 99.0 * one, -one
    for e in range(N_EXPERTS):
        e_lo = jnp.where(picked[e], jnp.minimum(e_lo, float(e)), e_lo)
        e_hi = jnp.where(picked[e], jnp.maximum(e_hi, float(e)), e_hi)
    a_lo, a_hi = zero, zero
    for e in range(N_EXPERTS):
        a_lo = jnp.where(picked[e] & (e_lo == float(e)), a[e], a_lo)
        a_hi = jnp.where(picked[e] & (e_hi == float(e)), a[e], a_hi)
    denom = a_lo + a_hi
    eid = jnp.concatenate([e_lo, e_hi], axis=0).astype(I32)
    w8 = jnp.concatenate([a_lo / denom, a_hi / denom] + [zero] * 6, axis=0)
    return eid, w8.T


def _router_specs(tm, token_block):
    const = lambda *_: (0, 0)
    ins = [pl.BlockSpec((N_EXPERTS, D_MODEL), const), pl.BlockSpec((N_EXPERTS, 1), const)]
    outs = [pl.BlockSpec((TOP_K, tm), lambda *g: (0, token_block(*g))),
            pl.BlockSpec((tm, 8), lambda *g: (token_block(*g), 0))]
    shapes = [jax.ShapeDtypeStruct((TOP_K, TOKENS), I32), jax.ShapeDtypeStruct((TOKENS, 8), F32)]
    return ins, outs, shapes


PLAN_CHUNK = 512


def _plan_kernel(eid_ref, pos_ref, meta_ref):
    n_chunks = TOKENS // PLAN_CHUNK
    erow = lax.broadcasted_iota(I32, (N_EXPERTS, PLAN_CHUNK), 0)

    def lanes(c):
        return pl.ds(pl.multiple_of(c * PLAN_CHUNK, PLAN_CHUNK), PLAN_CHUNK)

    def onehot(k, c):
        return erow == eid_ref[k:k + 1, lanes(c)]

    counts = jnp.zeros((N_EXPERTS, 1), F32)
    for k in range(TOP_K):
        counts = lax.fori_loop(
            0, n_chunks,
            lambda c, cnt, k=k: cnt + jnp.sum(jnp.where(onehot(k, c), 1.0, 0.0), axis=1, keepdims=True),
            counts)
    padded = jnp.ceil(counts * (1.0 / EXPERT_TM)) * EXPERT_TM
    ei = lax.broadcasted_iota(I32, (N_EXPERTS, N_EXPERTS), 0)
    ej = lax.broadcasted_iota(I32, (N_EXPERTS, N_EXPERTS), 1)
    padded_row = jnp.sum(jnp.where(ei == ej, padded, 0.0), axis=0, keepdims=True)
    start = jnp.sum(jnp.where(ej < ei, padded_row, 0.0), axis=1, keepdims=True)
    end = start + padded
    total = jnp.sum(padded, axis=0, keepdims=True)

    tri = jnp.where(lax.broadcasted_iota(I32, (PLAN_CHUNK, PLAN_CHUNK), 0)
                    <= lax.broadcasted_iota(I32, (PLAN_CHUNK, PLAN_CHUNK), 1), 1.0, 0.0).astype(BF16)

    def pos_body(c, carry, k):
        oh = onehot(k, c)
        ohf = jnp.where(oh, 1.0, 0.0)
        prefix = _dot(ohf.astype(BF16), tri)
        pos = jnp.sum(jnp.where(oh, start + carry + prefix - 1.0, 0.0), axis=0, keepdims=True)
        pos_ref[k:k + 1, lanes(c)] = pos.astype(I32)
        return carry + jnp.sum(ohf, axis=1, keepdims=True)

    carry = jnp.zeros((N_EXPERTS, 1), F32)
    for k in range(TOP_K):
        carry = lax.fori_loop(0, n_chunks, functools.partial(pos_body, k=k), carry)

    tile0 = lax.broadcasted_iota(I32, (1, META_LANES), 1).astype(F32) * EXPERT_TM
    n_used = total * (1.0 / EXPERT_TM)
    last = n_used - 1.0
    tile_c = jnp.minimum(tile0, last * EXPERT_TM)
    expert = jnp.sum(jnp.where(end <= tile_c, 1.0, 0.0), axis=0, keepdims=True)
    block = tile_c * (1.0 / EXPERT_TM)
    efl = lax.broadcasted_iota(I32, (N_EXPERTS, META_LANES), 0).astype(F32)
    nonempty = padded > 0.0
    nxt = jnp.min(jnp.where((efl > expert) & nonempty, efl, float(N_EXPERTS)), axis=0, keepdims=True)
    ordinal = jnp.sum(jnp.where((end <= tile_c) & nonempty, 1.0, 0.0), axis=0, keepdims=True)
    slot = ordinal - 2.0 * jnp.floor(ordinal * 0.5)
    meta = jnp.concatenate([expert, block, jnp.broadcast_to(n_used, (1, META_LANES)), nxt, slot]
                           + [jnp.zeros((1, META_LANES), F32)] * 3, axis=0)
    meta_ref[...] = meta.astype(I32)


def _plan(eid):
    return pl.pallas_call(
        _plan_kernel,
        in_specs=[pl.BlockSpec(memory_space=pltpu.VMEM)],
        out_specs=[pl.BlockSpec(memory_space=pltpu.VMEM), pl.BlockSpec(memory_space=pltpu.VMEM)],
        out_shape=[
            jax.ShapeDtypeStruct((TOP_K, TOKENS), I32),
            jax.ShapeDtypeStruct((8, META_LANES), I32),
        ],
        name="plan",
    )(eid)


def _sc_mesh():
    return plsc.VectorSubcoreMesh(core_axis_name="c", subcore_axis_name="s")


def _sc_worker_id():
    return lax.axis_index("s") * SC_CORES + lax.axis_index("c")


TOKEN_CHUNKS = TOKENS * PIECES // SC_CHUNK
CHUNKS_PER_WORKER = TOKEN_CHUNKS // SC_WORKERS


def _dispatch(words, didx):
    @functools.partial(
        pl.kernel,
        out_type=jax.ShapeDtypeStruct((SORTED_ROWS * PIECES, PIECE), U32),
        mesh=_sc_mesh(),
        scratch_types=[
            pltpu.VMEM((TOP_K, CHUNKS_PER_WORKER, SC_CHUNK), I32),
            pltpu.VMEM((DISPATCH_BUFS, SC_CHUNK, PIECE), U32),
            pltpu.SemaphoreType.DMA((DISPATCH_BUFS,)),
            pltpu.SemaphoreType.DMA((DISPATCH_BUFS, TOP_K)),
        ],
        name="moe_dispatch",
    )
    def run(words_hbm, didx_hbm, out_hbm, idx_v, bufs, rsem, wsem):
        c0 = _sc_worker_id() * CHUNKS_PER_WORKER
        for k in range(TOP_K):
            pltpu.sync_copy(didx_hbm.at[k, pl.ds(c0, CHUNKS_PER_WORKER)], idx_v.at[k])

        def read(c):
            b = c % DISPATCH_BUFS
            return pltpu.async_copy(words_hbm.at[pl.ds((c0 + c) * SC_CHUNK, SC_CHUNK)], bufs.at[b], rsem.at[b])

        def scatter(c):
            b = c % DISPATCH_BUFS
            return [pltpu.async_copy(bufs.at[b], out_hbm.at[idx_v.at[k, c]], wsem.at[b, k]) for k in range(TOP_K)]

        reads = {c: read(c) for c in range(SC_LOOKAHEAD)}
        scatters = {}
        for c in range(CHUNKS_PER_WORKER):
            if c >= DISPATCH_BUFS - SC_LOOKAHEAD:
                for d in scatters.pop(c - (DISPATCH_BUFS - SC_LOOKAHEAD)):
                    d.wait()
            if c + SC_LOOKAHEAD < CHUNKS_PER_WORKER:
                reads[c + SC_LOOKAHEAD] = read(c + SC_LOOKAHEAD)
            reads.pop(c).wait()
            scatters[c] = scatter(c)
        for ds in scatters.values():
            for d in ds:
                d.wait()

    return run(words, didx)


def _combine(sorted_out, didx):
    @functools.partial(
        pl.kernel,
        out_type=jax.ShapeDtypeStruct((TOP_K, TOKENS * PIECES, PIECE), U32),
        mesh=_sc_mesh(),
        scratch_types=[
            pltpu.VMEM((TOP_K, CHUNKS_PER_WORKER, SC_CHUNK), I32),
            pltpu.VMEM((COMBINE_BUFS, TOP_K, SC_CHUNK, PIECE), U32),
            pltpu.SemaphoreType.DMA((COMBINE_BUFS, TOP_K)),
            pltpu.SemaphoreType.DMA((COMBINE_BUFS, TOP_K)),
        ],
        name="moe_combine",
    )
    def run(src_hbm, didx_hbm, out_hbm, idx_v, bufs, rsem, wsem):
        c0 = _sc_worker_id() * CHUNKS_PER_WORKER
        for k in range(TOP_K):
            pltpu.sync_copy(didx_hbm.at[k, pl.ds(c0, CHUNKS_PER_WORKER)], idx_v.at[k])

        def gather(c):
            b = c % COMBINE_BUFS
            return [pltpu.async_copy(src_hbm.at[idx_v.at[k, c]], bufs.at[b, k], rsem.at[b, k]) for k in range(TOP_K)]

        def write(c):
            b = c % COMBINE_BUFS
            return [pltpu.async_copy(bufs.at[b, k], out_hbm.at[k, pl.ds((c0 + c) * SC_CHUNK, SC_CHUNK)],
                                     wsem.at[b, k]) for k in range(TOP_K)]

        gathers = {c: gather(c) for c in range(SC_LOOKAHEAD)}
        writes = {}
        for c in range(CHUNKS_PER_WORKER):
            if c >= COMBINE_BUFS - SC_LOOKAHEAD:
                for d in writes.pop(c - (COMBINE_BUFS - SC_LOOKAHEAD)):
                    d.wait()
            if c + SC_LOOKAHEAD < CHUNKS_PER_WORKER:
                gathers[c + SC_LOOKAHEAD] = gather(c + SC_LOOKAHEAD)
            for d in gathers.pop(c):
                d.wait()
            writes[c] = write(c)
        for ds in writes.values():
            for d in ds:
                d.wait()

    return run(sorted_out, didx)


def _expert_kernel(layer, te_ref, tb_ref, nu_ref, nx_ref, sl_ref, xs_ref, wg_hbm, wu_hbm, wd_hbm, out_ref,
                   wbuf, wgub, wdb, sem):
    i = pl.program_id(0)
    expert = te_ref[i]
    slot = sl_ref[i]
    fresh = (i == 0) | (expert != te_ref[jnp.maximum(i - 1, 0)])

    def weight_copies(e, s):
        return [pltpu.make_async_copy(w.at[layer, e], wbuf.at[s, j], sem.at[s, j])
                for j, w in enumerate((wg_hbm, wu_hbm, wd_hbm))]

    @pl.when(i == 0)
    def _():
        for c in weight_copies(expert, slot):
            c.start()

    @pl.when(fresh)
    def _():
        for c in weight_copies(expert, slot):
            c.wait()
        nxt = nx_ref[i]

        @pl.when(nxt < N_EXPERTS)
        def _():
            for c in weight_copies(nxt, 1 - slot):
                c.start()

        wgub[:, :D_MODEL] = wbuf[slot, 0].astype(BF16)
        wgub[:, D_MODEL:] = wbuf[slot, 1].astype(BF16)
        wdb[...] = wbuf[slot, 2].astype(BF16)

    @pl.when(i < nu_ref[0])
    def _():
        lo, hi = _unpack_words(_from_storage(xs_ref[...], EXPERT_TM))
        x = jnp.concatenate([lo.astype(BF16), hi.astype(BF16)], axis=1)
        hgu = _dot(x, wgub[...])
        hg, hu = hgu[:, :D_MODEL], hgu[:, D_MODEL:]
        h = (hg / (1.0 + jnp.exp(-hg)) * hu).astype(BF16)
        y = _dot(h, wdb[...])
        out_ref[...] = _to_storage(_pack_words(y), EXPERT_TM)


def _experts(meta, xs, w_gate, w_up, w_down, layer):
    te, tb, nu, nx, sl = (meta[r, :N_TILES] for r in range(5))
    wspec = pl.BlockSpec(memory_space=pl.ANY)
    rows = pl.BlockSpec((EXPERT_TM * PIECES, PIECE), lambda i, te, tb, nu, nx, sl: (tb[i], 0))
    return pl.pallas_call(
        functools.partial(_expert_kernel, layer),
        grid_spec=pltpu.PrefetchScalarGridSpec(
            num_scalar_prefetch=5,
            grid=(N_TILES,),
            in_specs=[rows, wspec, wspec, wspec],
            out_specs=rows,
            scratch_shapes=[
                pltpu.VMEM((2, 3, D_MODEL, D_MODEL), F32),
                pltpu.VMEM((D_MODEL, 2 * D_MODEL), BF16),
                pltpu.VMEM((D_MODEL, D_MODEL), BF16),
                pltpu.SemaphoreType.DMA((2, 3)),
            ],
        ),
        out_shape=jax.ShapeDtypeStruct((SORTED_ROWS * PIECES, PIECE), U32),
        compiler_params=_params(("arbitrary",), V7X_VMEM_LIMIT),
        name="experts",
    )(te, tb, nu, nx, sl, xs, w_gate, w_up, w_down)


COMBINE_TM = 512


def _combine_ln_kernel(x_ref, y_ref, w_ref, g_ref, b_ref, out_ref):
    w = w_ref[...]
    ffn = None
    for k in range(TOP_K):
        lo, hi = _unpack_words(_from_storage(y_ref[k], TOKEN_B))
        yk = w[:, k:k + 1] * jnp.concatenate([lo, hi], axis=1)
        ffn = yk if ffn is None else ffn + yk
    out_ref[...] = _layer_norm(ALPHA * x_ref[...] + ffn, g_ref[...], b_ref[...])


def _combine_ln(x1, y_tok, wcol, g, b):
    tm = COMBINE_TM
    return pl.pallas_call(
        _combine_ln_kernel,
        grid=(TOKENS // tm,),
        in_specs=[
            pl.BlockSpec((tm, D_MODEL), lambda i: (i, 0)),
            pl.BlockSpec((TOP_K, tm * PIECES, PIECE), lambda i: (0, i, 0)),
            pl.BlockSpec((tm, 8), lambda i: (i, 0)),
            pl.BlockSpec((1, D_MODEL), lambda i: (0, 0)),
            pl.BlockSpec((1, D_MODEL), lambda i: (0, 0)),
        ],
        out_specs=pl.BlockSpec((tm, D_MODEL), lambda i: (i, 0)),
        out_shape=jax.ShapeDtypeStruct((TOKENS, D_MODEL), F32),
        compiler_params=_params(("parallel",), V7X_VMEM_LIMIT),
        name="combine_ln",
    )(x1, y_tok, wcol, g, b)


def _sorted_storage_rows(pos):
    p = pos.reshape(TOP_K, TOKENS // TOKEN_B, 1, TOKEN_B)
    j = jnp.arange(PIECES, dtype=I32).reshape(1, 1, PIECES, 1)
    r = (p // EXPERT_TM) * (PIECES * EXPERT_TM) + j * EXPERT_TM + p % EXPERT_TM
    return r.reshape(TOP_K, TOKEN_CHUNKS, SC_CHUNK)


def kernel(x, rel_bias, w_in, w_gla_gate, b_gla_gate, gla_norm, diff_lambda, diff_norm, w_out_mix, w_pool,
           pool_scale, ln_g, ln_b, w_router, router_bias, w_gate, w_up, w_down):
    x2 = x.reshape(TOKENS, D_MODEL)
    bias_tiles = _bias_tiles(rel_bias)
    w_router_t = w_router.T
    router_bias_col = router_bias.reshape(N_EXPERTS, 1)
    row = lambda v: v.reshape(1, -1)
    col_scale = jnp.concatenate([
        jnp.full((GLA_KEY,), GLA_HEAD_K ** -0.5, F32), jnp.ones((GLA_KEY + 2 * GLA_WIDTH,), F32),
        jnp.full((DIFF_WIDTH,), DIFF_HEAD_DIM ** -0.5 * LOG2E, F32), jnp.ones((2 * DIFF_WIDTH,), F32)])
    for layer in range(DEPTH):
        if layer % 2 == 0:
            i = layer // 2
            wi = w_in[i]
            w_main = (jnp.concatenate([wi[:, :1536], wi[:, 1568:]], axis=1) * col_scale).astype(BF16)
            w_lr = jnp.pad(wi[:, 1536:1568], ((0, 0), (0, LR_COLS - 2 * GATE_RANK))).astype(BF16)
            hb, lr = _inproj(x2, w_main, w_lr)
            wg_pad = jnp.zeros((2, LR_COLS, GLA_KEY), F32)
            wg_pad = wg_pad.at[0, 0:GATE_RANK].set(w_gla_gate[i, 0])
            wg_pad = wg_pad.at[1, GATE_RANK:2 * GATE_RANK].set(w_gla_gate[i, 1])
            o_gla = _gla(hb, lr, wg_pad, b_gla_gate[i], row(gla_norm[i]))
            lam_init = 0.8 - 0.6 * math.exp(-0.3 * layer)
            o_diff = _diff(hb, bias_tiles, diff_lambda[i], row(diff_norm[i]), lam_init)
            x1, words, eid, wcol = _outproj_ln(o_gla, o_diff, w_out_mix[i].astype(BF16), x2,
                                               row(ln_g[layer, 0]), row(ln_b[layer, 0]),
                                               w_router_t, router_bias_col)
        else:
            j = layer // 2
            x1, words, eid, wcol = _pool_ln(x2.reshape(BATCH, SEQ, D_MODEL), w_pool[j].astype(BF16),
                                            row(pool_scale[j]), row(ln_g[layer, 0]), row(ln_b[layer, 0]),
                                            w_router_t, router_bias_col)
        pos, meta = _plan(eid)
        didx = _sorted_storage_rows(pos)
        xs = _dispatch(words, didx)
        ys = _experts(meta, xs, w_gate, w_up, w_down, layer)
        y_tok = _combine(ys, didx)
        x2 = _combine_ln(x1, y_tok, wcol, row(ln_g[layer, 1]), row(ln_b[layer, 1]))
    return x2.reshape(BATCH, SEQ, D_MODEL)
```
